```python
import jax, jax.numpy as jnp
from jax import lax
import numpy as np

D_MODEL = 2048
BATCH = 4
SEQ = 2048
DEPTH = 2
DEC_BATCH = 32
DEC_SEQ = 16
PAST_LEN = 1024

CHUNK = 64
N_MIXERS = 2
N_CONV_LAYERS = (DEPTH + 1) // 2
N_SGU_LAYERS = DEPTH // 2
CONV_W = 3
CONV_GROUPS = 16
SGU_CHUNK = 128
SGU_WIDTH = D_MODEL
SGU_GROUPS = 16
SGU_GROUP_DIM = SGU_WIDTH // SGU_GROUPS
D_FF = ((8 * D_MODEL // 3 + 255) // 256) * 256
EPS = 1e-6

kernel_name = "hybrid_shortconv_chunk_sgu_stream_step"


def rms_norm(x, g):
    xf = x.astype(jnp.float32)
    y = xf * lax.rsqrt(jnp.mean(xf * xf, axis=-1, keepdims=True) + EPS)
    return (y * g.astype(jnp.float32)).astype(x.dtype)


def layer_norm(x, g, b):
    xf = x.astype(jnp.float32)
    mu = jnp.mean(xf, axis=-1, keepdims=True)
    xc = xf - mu
    y = xc * lax.rsqrt(jnp.mean(xc * xc, axis=-1, keepdims=True) + EPS)
    return (y * g.astype(jnp.float32) + b.astype(jnp.float32)).astype(x.dtype)


def short_conv_mixer(h, hist, w_in, conv_w, w_out):
    L = h.shape[1]
    proj = h @ w_in
    gate_b, gate_c, z = jnp.split(proj, 3, axis=-1)
    cz = gate_c * z
    xp = jnp.concatenate([hist.astype(cz.dtype), cz], axis=1)
    conv = conv_w[0] * xp[:, 0:L] + conv_w[1] * xp[:, 1:L + 1] + conv_w[2] * xp[:, 2:L + 2]
    y = (gate_b * conv) @ w_out
    return y, xp[:, L:]


def chunk_sgu_mixer(h, w_in, b_in, ln_g, ln_b, w_s, b_s, w_out):
    bsz, L, _ = h.shape
    n = min(L, SGU_CHUNK)
    zz = jax.nn.gelu(h @ w_in + b_in, approximate=False)
    u, v = jnp.split(zz, 2, axis=-1)
    v = layer_norm(v, ln_g, ln_b)
    vb = v.reshape(bsz, L // n, n, SGU_GROUPS, SGU_GROUP_DIM)
    w = jnp.tril(w_s[:, :n, :n])
    mixed = jnp.einsum('gij,bcjgd->bcigd', w, vb) + b_s[:, :n].T[None, None, :, :, None]
    y = (u * mixed.reshape(bsz, L, SGU_WIDTH)) @ w_out
    return y, v


def swiglu_ffn(h, w_gate, w_up, w_down):
    return (jax.nn.silu(h @ w_gate) * (h @ w_up)) @ w_down


def setup_inputs(seed: int = 0) -> dict:
    key = jax.random.key(seed)
    ks = jax.random.split(key, 24)
    f32 = jnp.float32
    nrm = lambda k, shape, scale: jax.random.normal(k, shape, f32) * scale
    D = D_MODEL
    return {
        "x_prompt": nrm(ks[0], (BATCH, SEQ, D), 1.0),
        "x_sample": nrm(ks[1], (DEC_BATCH, DEC_SEQ, D), 1.0),
        "cache_conv": nrm(ks[2], (N_CONV_LAYERS, DEC_BATCH, CONV_W - 1, D), 1.0),
        "norm_mix_pre": 1.0 + nrm(ks[3], (DEPTH, D), 0.05),
        "norm_mix_post": 1.0 + nrm(ks[4], (DEPTH, D), 0.05),
        "norm_ffn_pre": 1.0 + nrm(ks[5], (DEPTH, D), 0.05),
        "norm_ffn_post": 1.0 + nrm(ks[6], (DEPTH, D), 0.05),
        "a_w_in": nrm(ks[7], (N_CONV_LAYERS, D, 3 * D), D ** -0.5),
        "a_conv_w": nrm(ks[8], (N_CONV_LAYERS, CONV_W, D), CONV_W ** -0.5),
        "a_w_out": nrm(ks[9], (N_CONV_LAYERS, D, D), D ** -0.5),
        "b_w_in": nrm(ks[10], (N_SGU_LAYERS, D, 2 * SGU_WIDTH), D ** -0.5),
        "b_b_in": nrm(ks[11], (N_SGU_LAYERS, 2 * SGU_WIDTH), 0.02),
        "b_ln_g": 1.0 + nrm(ks[12], (N_SGU_LAYERS, SGU_WIDTH), 0.05),
        "b_ln_b": nrm(ks[13], (N_SGU_LAYERS, SGU_WIDTH), 0.02),
        "b_w_s": nrm(ks[14], (N_SGU_LAYERS, SGU_GROUPS, SGU_CHUNK, SGU_CHUNK), SGU_CHUNK ** -0.5),
        "b_b_s": 1.0 + nrm(ks[15], (N_SGU_LAYERS, SGU_GROUPS, SGU_CHUNK), 0.1),
        "b_w_out": nrm(ks[16], (N_SGU_LAYERS, SGU_WIDTH, D), SGU_WIDTH ** -0.5),
        "ffn_w_gate": nrm(ks[17], (DEPTH, D, D_FF), D ** -0.5),
        "ffn_w_up": nrm(ks[18], (DEPTH, D, D_FF), D ** -0.5),
        "ffn_w_down": nrm(ks[19], (DEPTH, D_FF, D), D_FF ** -0.5),
    }


def reference(x_prompt, x_sample, cache_conv, norm_mix_pre, norm_mix_post, norm_ffn_pre,
              norm_ffn_post, a_w_in, a_conv_w, a_w_out, b_w_in, b_b_in, b_ln_g, b_ln_b,
              b_w_s, b_b_s, b_w_out, ffn_w_gate, ffn_w_up, ffn_w_down):
    xp, xs = x_prompt, x_sample
    conv_prompt, conv_sample, sgu_sample = [], [], []
    for i in range(DEPTH):
        j = i // N_MIXERS
        hp = rms_norm(xp, norm_mix_pre[i])
        hs = rms_norm(xs, norm_mix_pre[i])
        if i % N_MIXERS == 0:
            zero_hist = jnp.zeros((xp.shape[0], CONV_W - 1, D_MODEL), xp.dtype)
            mp, st_p = short_conv_mixer(hp, zero_hist, a_w_in[j], a_conv_w[j], a_w_out[j])
            ms, st_s = short_conv_mixer(hs, cache_conv[j], a_w_in[j], a_conv_w[j], a_w_out[j])
            conv_prompt.append(st_p)
            conv_sample.append(st_s)
        else:
            mp, _ = chunk_sgu_mixer(hp, b_w_in[j], b_b_in[j], b_ln_g[j], b_ln_b[j],
                                    b_w_s[j], b_b_s[j], b_w_out[j])
            ms, v_s = chunk_sgu_mixer(hs, b_w_in[j], b_b_in[j], b_ln_g[j], b_ln_b[j],
                                      b_w_s[j], b_b_s[j], b_w_out[j])
            sgu_sample.append(v_s)
        xp = xp + rms_norm(mp, norm_mix_post[i])
        xs = xs + rms_norm(ms, norm_mix_post[i])
        xp = xp + rms_norm(swiglu_ffn(rms_norm(xp, norm_ffn_pre[i]), ffn_w_gate[i], ffn_w_up[i],
                                      ffn_w_down[i]), norm_ffn_post[i])
        xs = xs + rms_norm(swiglu_ffn(rms_norm(xs, norm_ffn_pre[i]), ffn_w_gate[i], ffn_w_up[i],
                                      ffn_w_down[i]), norm_ffn_post[i])
    state_conv_prompt = jnp.stack(conv_prompt)
    state_conv_sample = jnp.stack(conv_sample)
    state_sgu_v_sample = jnp.stack(sgu_sample)
    return (xp, xs, state_conv_prompt, state_conv_sample, state_sgu_v_sample)
```

```python
import functools
import math

import jax
import jax.numpy as jnp
from jax import lax
from jax.experimental import pallas as pl
from jax.experimental.pallas import tpu as pltpu

EPS = 1e-6
CONV_W = 3
SGU_CHUNK = 128
SGU_GROUP_DIM = 128
SUBLANES = 8

MIB = 1024 * 1024
VMEM_LIMIT_BYTES = 56 * MIB

F32 = jnp.float32
BF16 = jnp.bfloat16


def _params(n_axes):
    return pltpu.CompilerParams(
        dimension_semantics=("arbitrary",) * n_axes,
        vmem_limit_bytes=VMEM_LIMIT_BYTES)


def _rms(x, g):
    y = x * lax.rsqrt(jnp.mean(x * x, axis=-1, keepdims=True) + EPS)
    return y * g


def _dot(a, b):
    return jnp.dot(a, b, preferred_element_type=F32)


def _resident(block_shape, index_map):
    return pl.BlockSpec(block_shape, index_map, pipeline_mode=pl.Buffered(1))


def _prenorm_kernel(xp_ref, xs_ref, g_ref, h_ref, *, n_prompt_blocks):
    m = pl.program_id(0)

    @pl.when(m < n_prompt_blocks)
    def _():
        h_ref[...] = _rms(xp_ref[...], g_ref[...]).astype(h_ref.dtype)

    @pl.when(m >= n_prompt_blocks)
    def _():
        h_ref[...] = _rms(xs_ref[...], g_ref[...]).astype(h_ref.dtype)


def _prenorm(xp, xs, g, *, tm):
    mp, d = xp.shape
    ms = xs.shape[0]
    npb, nsb = mp // tm, ms // tm
    return pl.pallas_call(
        functools.partial(_prenorm_kernel, n_prompt_blocks=npb),
        grid=(npb + nsb,),
        in_specs=[
            pl.BlockSpec((tm, d), lambda m: (jnp.minimum(m, npb - 1), 0)),
            pl.BlockSpec((tm, d), lambda m: (jnp.maximum(m - npb, 0), 0)),
            _resident((1, d), lambda m: (0, 0)),
        ],
        out_specs=pl.BlockSpec((tm, d), lambda m: (m, 0)),
        out_shape=jax.ShapeDtypeStruct((mp + ms, d), BF16),
        compiler_params=_params(1),
        name="prenorm",
    )(xp, xs, g)


def _conv_in_kernel(h_ref, wb_ref, wc_ref, wz_ref, cw_ref, hist_ref,
                    p_ref, tail_ref, czs_ref, carry_ref,
                    *, n_prompt_blocks, blocks_per_seq, dec_seq):
    m = pl.program_id(1)
    tm, tn = p_ref.shape
    h = h_ref[...]
    gate_b = _dot(h, wb_ref[...])
    cz = _dot(h, wc_ref[...]) * _dot(h, wz_ref[...])

    @pl.when(m % blocks_per_seq == 0)
    def _():
        carry_ref[...] = jnp.zeros_like(carry_ref)

    prev = carry_ref[...]
    prev1 = prev[SUBLANES - 1:SUBLANES, :]
    prev2 = prev[SUBLANES - 2:SUBLANES - 1, :]
    row = lax.broadcasted_iota(jnp.int32, (tm, tn), 0)
    cz1 = jnp.where(row == 0, prev1, pltpu.roll(cz, 1, 0))
    cz2 = jnp.where(row == 0, prev2,
                    jnp.where(row == 1, prev1, pltpu.roll(cz, 2, 0)))

    is_sample = m >= n_prompt_blocks
    hist = hist_ref[...]
    pos = row % dec_seq
    cz1 = jnp.where(is_sample & (pos == 0), pltpu.roll(hist, tm - 1, 0), cz1)
    cz2 = jnp.where(is_sample & (pos < CONV_W - 1), hist, cz2)

    cw = cw_ref[...]
    conv = cw[0:1, :] * cz2 + cw[1:2, :] * cz1 + cw[2:3, :] * cz
    p_ref[...] = (gate_b * conv).astype(p_ref.dtype)
    carry_ref[...] = cz[tm - SUBLANES:, :]

    @pl.when((m < n_prompt_blocks) & (m % blocks_per_seq == blocks_per_seq - 1))
    def _():
        tail_ref[0] = cz[tm - SUBLANES:, :]

    @pl.when(is_sample)
    def _():
        czs_ref[...] = cz


def _conv_in(h, w_in, conv_w, hist, *, mp, seq, dec_seq, tm, tn):
    m_all, d = h.shape
    ms = m_all - mp
    npb, nsb, bps = mp // tm, ms // tm, seq // tm
    n_seq = mp // seq
    nt = d // tn
    kernel = functools.partial(_conv_in_kernel, n_prompt_blocks=npb,
                               blocks_per_seq=bps, dec_seq=dec_seq)
    sample_block = lambda n, m: (jnp.maximum(m - npb, 0), n)
    return pl.pallas_call(
        kernel,
        grid=(nt, npb + nsb),
        in_specs=[
            pl.BlockSpec((tm, d), lambda n, m: (m, 0)),
            pl.BlockSpec((d, tn), lambda n, m: (0, n)),
            pl.BlockSpec((d, tn), lambda n, m: (0, nt + n)),
            pl.BlockSpec((d, tn), lambda n, m: (0, 2 * nt + n)),
            pl.BlockSpec((CONV_W, tn), lambda n, m: (0, n)),
            pl.BlockSpec((tm, tn), sample_block),
        ],
        out_specs=[
            pl.BlockSpec((tm, tn), lambda n, m: (m, n)),
            pl.BlockSpec((1, SUBLANES, tn),
                         lambda n, m: (jnp.minimum(m, npb - 1) // bps, 0, n)),
            pl.BlockSpec((tm, tn), sample_block),
        ],
        out_shape=[
            jax.ShapeDtypeStruct((m_all, d), BF16),
            jax.ShapeDtypeStruct((n_seq, SUBLANES, d), F32),
            jax.ShapeDtypeStruct((ms, d), F32),
        ],
        scratch_shapes=[pltpu.VMEM((SUBLANES, tn), F32)],
        compiler_params=_params(2),
        name="conv_in",
    )(h, w_in, w_in, w_in, conv_w, hist)


def _residual_tail(y, x, gpost, gnext):
    xn = x + _rms(y, gpost)
    return xn, _rms(xn, gnext)


def _proj_out_kernel(*refs, n_prompt_blocks, split_res, final):
    m = pl.program_id(0)
    refs = list(refs)
    a_ref, w_ref = refs[:2]
    refs = refs[2:]
    if split_res:
        xp_ref, xs_ref = refs[:2]
        refs = refs[2:]
    else:
        x_ref = refs[0]
        refs = refs[1:]
    gpost_ref = refs[0]
    refs = refs[1:]
    if not final:
        gnext_ref = refs[0]
        refs = refs[1:]
    o0_ref, o1_ref = refs

    y = _dot(a_ref[...], w_ref[...])

    def finish(x):
        return _residual_tail(y, x, gpost_ref[...], gnext_ref[...])

    if split_res:
        @pl.when(m < n_prompt_blocks)
        def _():
            xn, hn = finish(xp_ref[...])
            o0_ref[...] = xn
            o1_ref[...] = hn.astype(o1_ref.dtype)

        @pl.when(m >= n_prompt_blocks)
        def _():
            xn, hn = finish(xs_ref[...])
            o0_ref[...] = xn
            o1_ref[...] = hn.astype(o1_ref.dtype)
    elif final:
        xn = x_ref[...] + _rms(y, gpost_ref[...])

        @pl.when(m < n_prompt_blocks)
        def _():
            o0_ref[...] = xn

        @pl.when(m >= n_prompt_blocks)
        def _():
            o1_ref[...] = xn
    else:
        xn, hn = finish(x_ref[...])
        o0_ref[...] = xn
        o1_ref[...] = hn.astype(o1_ref.dtype)


def _proj_out(a, w, res, gpost, gnext, *, mp, tm):
    final = gnext is None
    m_all, k = a.shape
    d = w.shape[1]
    ms = m_all - mp
    npb, nsb = mp // tm, ms // tm
    split_res = isinstance(res, tuple)
    prompt_block = lambda m: (jnp.minimum(m, npb - 1), 0)
    sample_block = lambda m: (jnp.maximum(m - npb, 0), 0)
    row_block = lambda m: (m, 0)
    in_specs = [pl.BlockSpec((tm, k), row_block),
                _resident((k, d), lambda m: (0, 0))]
    if split_res:
        in_specs += [pl.BlockSpec((tm, d), prompt_block),
                     pl.BlockSpec((tm, d), sample_block)]
        res_args = res
    else:
        in_specs += [pl.BlockSpec((tm, d), row_block)]
        res_args = (res,)
    norm_args = (gpost,) if final else (gpost, gnext)
    in_specs += [_resident((1, d), lambda m: (0, 0)) for _ in norm_args]
    if final:
        out_specs = [pl.BlockSpec((tm, d), prompt_block),
                     pl.BlockSpec((tm, d), sample_block)]
        out_shape = [jax.ShapeDtypeStruct((mp, d), F32),
                     jax.ShapeDtypeStruct((ms, d), F32)]
    else:
        out_specs = [pl.BlockSpec((tm, d), row_block),
                     pl.BlockSpec((tm, d), row_block)]
        out_shape = [jax.ShapeDtypeStruct((m_all, d), F32),
                     jax.ShapeDtypeStruct((m_all, d), BF16)]
    kernel = functools.partial(_proj_out_kernel, n_prompt_blocks=npb,
                               split_res=split_res, final=final)
    return pl.pallas_call(
        kernel,
        grid=(npb + nsb,),
        in_specs=in_specs,
        out_specs=out_specs,
        out_shape=out_shape,
        compiler_params=_params(1),
        name="proj_out_k%d" % k,
    )(a, w, *res_args, *norm_args)


def _ffn_up_kernel(h_ref, wg_ref, wu_ref, a_ref):
    h = h_ref[...]
    g = _dot(h, wg_ref[...])
    u = _dot(h, wu_ref[...])
    a_ref[...] = (g * jax.nn.sigmoid(g) * u).astype(a_ref.dtype)


def _ffn_up(h, w_gate, w_up, *, tm, tn):
    m_all, d = h.shape
    f = w_gate.shape[1]
    return pl.pallas_call(
        _ffn_up_kernel,
        grid=(f // tn, m_all // tm),
        in_specs=[
            pl.BlockSpec((tm, d), lambda n, m: (m, 0)),
            pl.BlockSpec((d, tn), lambda n, m: (0, n)),
            pl.BlockSpec((d, tn), lambda n, m: (0, n)),
        ],
        out_specs=pl.BlockSpec((tm, tn), lambda n, m: (m, n)),
        out_shape=jax.ShapeDtypeStruct((m_all, f), BF16),
        compiler_params=_params(2),
        name="ffn_up",
    )(h, w_gate, w_up)


def _sgu_in_kernel(h_ref, w_ref, b_ref, zz_ref):
    z = _dot(h_ref[...], w_ref[...]) + b_ref[...]
    gelu = 0.5 * z * (1.0 + lax.erf(z * math.sqrt(0.5)))
    zz_ref[...] = gelu.astype(zz_ref.dtype)


def _sgu_in(h, w, b, *, tm, tn):
    m_all, d = h.shape
    n_out = w.shape[1]
    return pl.pallas_call(
        _sgu_in_kernel,
        grid=(n_out // tn, m_all // tm),
        in_specs=[
            pl.BlockSpec((tm, d), lambda n, m: (m, 0)),
            pl.BlockSpec((d, tn), lambda n, m: (0, n)),
            pl.BlockSpec((1, tn), lambda n, m: (0, n)),
        ],
        out_specs=pl.BlockSpec((tm, tn), lambda n, m: (m, n)),
        out_shape=jax.ShapeDtypeStruct((m_all, n_out), BF16),
        compiler_params=_params(2),
        name="sgu_in",
    )(h, w, b)


def _sgu_out_kernel(u_ref, v_ref, lng_ref, lnb_ref, wmix_ref, bmix_ref, w_ref,
                    x_ref, gpost_ref, gnext_ref,
                    xo_ref, ho_ref, vn_ref, yin_ref,
                    *, n_prompt_blocks, dec_seq):
    m = pl.program_id(0)
    tm, d = x_ref.shape
    n_groups = d // SGU_GROUP_DIM
    is_sample = m >= n_prompt_blocks
    kind = is_sample.astype(jnp.int32)

    v = v_ref[...].astype(F32)
    mu = jnp.mean(v, axis=-1, keepdims=True)
    vc = v - mu
    vn = vc * lax.rsqrt(jnp.mean(vc * vc, axis=-1, keepdims=True) + EPS)
    vn = vn * lng_ref[...] + lnb_ref[...]

    @pl.when(is_sample)
    def _():
        vn_ref[...] = vn

    vnb = vn.astype(BF16)

    i = lax.broadcasted_iota(jnp.int32, (SGU_CHUNK, SGU_CHUNK), 0)
    j = lax.broadcasted_iota(jnp.int32, (SGU_CHUNK, SGU_CHUNK), 1)
    same_seq = (i // dec_seq) == (j // dec_seq)
    mask = (i >= j) & (same_seq | jnp.logical_not(is_sample))
    bias = bmix_ref[kind]
    for g in range(n_groups):
        cols = slice(g * SGU_GROUP_DIM, (g + 1) * SGU_GROUP_DIM)
        wg = jnp.where(mask, wmix_ref[kind, g], 0.0).astype(BF16)
        for c in range(tm // SGU_CHUNK):
            rows = slice(c * SGU_CHUNK, (c + 1) * SGU_CHUNK)
            mixed = _dot(wg, vnb[rows, cols]) + bias[:, cols]
            u = u_ref[rows, cols].astype(F32)
            yin_ref[rows, cols] = (u * mixed).astype(yin_ref.dtype)

    y = _dot(yin_ref[...], w_ref[...])
    xn, hn = _residual_tail(y, x_ref[...], gpost_ref[...], gnext_ref[...])
    xo_ref[...] = xn
    ho_ref[...] = hn.astype(ho_ref.dtype)


def _sgu_out(zz, ln_g, ln_b, wmix, bmix, w_out, x, gpost, gnext,
             *, mp, dec_seq, tm):
    m_all, d = x.shape
    ms = m_all - mp
    npb, nsb = mp // tm, ms // tm
    n_groups = d // SGU_GROUP_DIM
    row_block = lambda m: (m, 0)
    const2 = lambda m: (0, 0)
    kernel = functools.partial(_sgu_out_kernel, n_prompt_blocks=npb,
                               dec_seq=dec_seq)
    return pl.pallas_call(
        kernel,
        grid=(npb + nsb,),
        in_specs=[
            pl.BlockSpec((tm, d), lambda m: (m, 0)),
            pl.BlockSpec((tm, d), lambda m: (m, 1)),
            _resident((1, d), const2),
            _resident((1, d), const2),
            _resident((2, n_groups, SGU_CHUNK, SGU_CHUNK),
                      lambda m: (0, 0, 0, 0)),
            _resident((2, SGU_CHUNK, d), lambda m: (0, 0, 0)),
            _resident((d, d), const2),
            pl.BlockSpec((tm, d), row_block),
            _resident((1, d), const2),
            _resident((1, d), const2),
        ],
        out_specs=[
            pl.BlockSpec((tm, d), row_block),
            pl.BlockSpec((tm, d), row_block),
            pl.BlockSpec((tm, d), lambda m: (jnp.maximum(m - npb, 0), 0)),
        ],
        out_shape=[
            jax.ShapeDtypeStruct((m_all, d), F32),
            jax.ShapeDtypeStruct((m_all, d), BF16),
            jax.ShapeDtypeStruct((ms, d), F32),
        ],
        scratch_shapes=[pltpu.VMEM((tm, d), BF16)],
        compiler_params=_params(1),
        name="sgu_out",
    )(zz, zz, ln_g, ln_b, wmix, bmix, w_out, x, gpost, gnext)


TM_WIDE = 512
TM_TAIL = 256
TN_CONV = 512
TN_FFN = 1408
TN_SGU = 1024


def kernel(x_prompt, x_sample, cache_conv, norm_mix_pre, norm_mix_post, norm_ffn_pre, norm_ffn_post, a_w_in, a_conv_w, a_w_out, b_w_in, b_b_in, b_ln_g, b_ln_b, b_w_s, b_b_s, b_w_out, ffn_w_gate, ffn_w_up, ffn_w_down):
    batch, seq, d = x_prompt.shape
    dec_batch, dec_seq, _ = x_sample.shape
    depth = norm_mix_pre.shape[0]
    mp, ms = batch * seq, dec_batch * dec_seq
    n_groups = d // SGU_GROUP_DIM
    assert depth == 2 and mp % TM_WIDE == 0 and ms % TM_WIDE == 0
    assert seq % TM_WIDE == 0 and SGU_CHUNK % dec_seq == 0 and dec_seq >= CONV_W - 1
    assert seq % SGU_CHUNK == 0 and TM_TAIL % SGU_CHUNK == 0

    xp = x_prompt.reshape(mp, d)
    xs = x_sample.reshape(ms, d)
    row = lambda p, i: p[i].reshape(1, -1)

    h = _prenorm(xp, xs, row(norm_mix_pre, 0), tm=TM_WIDE)
    hist = jnp.pad(cache_conv[0], ((0, 0), (0, dec_seq - (CONV_W - 1)), (0, 0)))
    p, tails, cz_s = _conv_in(h, a_w_in[0].astype(BF16), a_conv_w[0],
                              hist.reshape(ms, d), mp=mp, seq=seq,
                              dec_seq=dec_seq, tm=TM_WIDE, tn=TN_CONV)
    state_conv_prompt = tails[None, :, SUBLANES - (CONV_W - 1):, :]
    state_conv_sample = cz_s.reshape(dec_batch, dec_seq, d)[None, :, dec_seq - (CONV_W - 1):, :]
    x, h = _proj_out(p, a_w_out[0].astype(BF16), (xp, xs), row(norm_mix_post, 0),
                     row(norm_ffn_pre, 0), mp=mp, tm=TM_TAIL)

    a = _ffn_up(h, ffn_w_gate[0].astype(BF16), ffn_w_up[0].astype(BF16),
                tm=TM_WIDE, tn=TN_FFN)
    x, h = _proj_out(a, ffn_w_down[0].astype(BF16), x, row(norm_ffn_post, 0),
                     row(norm_mix_pre, 1), mp=mp, tm=TM_TAIL)

    zz = _sgu_in(h, b_w_in[0].astype(BF16), b_b_in[0].reshape(1, -1),
                 tm=TM_WIDE, tn=TN_SGU)
    reps = SGU_CHUNK // dec_seq
    w_s = b_w_s[0]
    wmix = jnp.stack([w_s, jnp.tile(w_s[:, :dec_seq, :dec_seq], (1, reps, reps))])
    b_s = b_b_s[0]
    b_rows = jnp.stack([b_s, jnp.tile(b_s[:, :dec_seq], (1, reps))])
    bmix = jnp.repeat(jnp.swapaxes(b_rows, 1, 2), SGU_GROUP_DIM, axis=2)
    x, h, vn_s = _sgu_out(zz, b_ln_g[0].reshape(1, -1), b_ln_b[0].reshape(1, -1),
                          wmix, bmix, b_w_out[0].astype(BF16), x,
                          row(norm_mix_post, 1), row(norm_ffn_pre, 1),
                          mp=mp, dec_seq=dec_seq, tm=TM_TAIL)
    state_sgu_v_sample = vn_s.reshape(1, dec_batch, dec_seq, d)

    a = _ffn_up(h, ffn_w_gate[1].astype(BF16), ffn_w_up[1].astype(BF16),
                tm=TM_WIDE, tn=TN_FFN)
    yp, ys = _proj_out(a, ffn_w_down[1].astype(BF16), x, row(norm_ffn_post, 1),
                       None, mp=mp, tm=TM_TAIL)

    return (yp.reshape(batch, seq, d), ys.reshape(dec_batch, dec_seq, d),
            state_conv_prompt, state_conv_sample, state_sgu_v_sample)
```

```python
import functools
import math

import jax
import jax.numpy as jnp
from jax import lax
from jax.experimental import pallas as pl
from jax.experimental.pallas import tpu as pltpu

EPS = 1e-6
CONV_W = 3
SGU_CHUNK = 128
SGU_GROUP_DIM = 128
SUBLANES = 8

MIB = 1024 * 1024
VMEM_LIMIT_BYTES = 56 * MIB

F32 = jnp.float32
BF16 = jnp.bfloat16


def _params(n_axes):
    return pltpu.CompilerParams(
        dimension_semantics=("arbitrary",) * n_axes,
        vmem_limit_bytes=VMEM_LIMIT_BYTES)


def _rms(x, g):
    y = x * lax.rsqrt(jnp.mean(x * x, axis=-1, keepdims=True) + EPS)
    return y * g


def _dot(a, b):
    return jnp.dot(a, b, preferred_element_type=F32)


def _resident(block_shape, index_map):
    return pl.BlockSpec(block_shape, index_map, pipeline_mode=pl.Buffered(1))


def _prenorm_kernel(xp_ref, xs_ref, g_ref, h_ref, *, n_prompt_blocks):
    m = pl.program_id(0)

    @pl.when(m < n_prompt_blocks)
    def _():
        h_ref[...] = _rms(xp_ref[...], g_ref[...]).astype(h_ref.dtype)

    @pl.when(m >= n_prompt_blocks)
    def _():
        h_ref[...] = _rms(xs_ref[...], g_ref[...]).astype(h_ref.dtype)


def _prenorm(xp, xs, g, *, tm):
    mp, d = xp.shape
    ms = xs.shape[0]
    npb, nsb = mp // tm, ms // tm
    return pl.pallas_call(
        functools.partial(_prenorm_kernel, n_prompt_blocks=npb),
        grid=(npb + nsb,),
        in_specs=[
            pl.BlockSpec((tm, d), lambda m: (jnp.minimum(m, npb - 1), 0)),
            pl.BlockSpec((tm, d), lambda m: (jnp.maximum(m - npb, 0), 0)),
            _resident((1, d), lambda m: (0, 0)),
        ],
        out_specs=pl.BlockSpec((tm, d), lambda m: (m, 0)),
        out_shape=jax.ShapeDtypeStruct((mp + ms, d), BF16),
        compiler_params=_params(1),
        name="prenorm",
    )(xp, xs, g)


def _conv_in_kernel(h_ref, wb_ref, wc_ref, wz_ref, cw_ref, hist_ref,
                    p_ref, tail_ref, czs_ref, carry_ref, wbf_ref,
                    *, n_prompt_blocks, blocks_per_seq, dec_seq):
    m = pl.program_id(1)
    tm, tn = p_ref.shape

    @pl.when(m == 0)
    def _():
        wbf_ref[0] = wb_ref[...].astype(BF16)
        wbf_ref[1] = wc_ref[...].astype(BF16)
        wbf_ref[2] = wz_ref[...].astype(BF16)

    h = h_ref[...]
    gate_b = _dot(h, wbf_ref[0])
    cz = _dot(h, wbf_ref[1]) * _dot(h, wbf_ref[2])

    @pl.when(m % blocks_per_seq == 0)
    def _():
        carry_ref[...] = jnp.zeros_like(carry_ref)

    prev = carry_ref[...]
    prev1 = prev[SUBLANES - 1:SUBLANES, :]
    prev2 = prev[SUBLANES - 2:SUBLANES - 1, :]
    row = lax.broadcasted_iota(jnp.int32, (tm, tn), 0)
    cz1 = jnp.where(row == 0, prev1, pltpu.roll(cz, 1, 0))
    cz2 = jnp.where(row == 0, prev2,
                    jnp.where(row == 1, prev1, pltpu.roll(cz, 2, 0)))

    is_sample = m >= n_prompt_blocks
    hist = hist_ref[...]
    pos = row % dec_seq
    cz1 = jnp.where(is_sample & (pos == 0), pltpu.roll(hist, tm - 1, 0), cz1)
    cz2 = jnp.where(is_sample & (pos < CONV_W - 1), hist, cz2)

    cw = cw_ref[...]
    conv = cw[0:1, :] * cz2 + cw[1:2, :] * cz1 + cw[2:3, :] * cz
    p_ref[...] = (gate_b * conv).astype(p_ref.dtype)
    carry_ref[...] = cz[tm - SUBLANES:, :]

    @pl.when((m < n_prompt_blocks) & (m % blocks_per_seq == blocks_per_seq - 1))
    def _():
        tail_ref[0] = cz[tm - SUBLANES:, :]

    @pl.when(is_sample)
    def _():
        czs_ref[...] = cz


def _conv_in(h, w_in, conv_w, hist, *, layer, mp, seq, dec_seq, tm, tn):
    m_all, d = h.shape
    ms = m_all - mp
    npb, nsb, bps = mp // tm, ms // tm, seq // tm
    n_seq = mp // seq
    nt = d // tn
    kernel = functools.partial(_conv_in_kernel, n_prompt_blocks=npb,
                               blocks_per_seq=bps, dec_seq=dec_seq)
    sample_block = lambda n, m: (jnp.maximum(m - npb, 0), n)
    return pl.pallas_call(
        kernel,
        grid=(nt, npb + nsb),
        in_specs=[
            pl.BlockSpec((tm, d), lambda n, m: (m, 0)),
            pl.BlockSpec((None, d, tn), lambda n, m: (layer, 0, n)),
            pl.BlockSpec((None, d, tn), lambda n, m: (layer, 0, nt + n)),
            pl.BlockSpec((None, d, tn), lambda n, m: (layer, 0, 2 * nt + n)),
            pl.BlockSpec((CONV_W, tn), lambda n, m: (0, n)),
            pl.BlockSpec((tm, tn), sample_block),
        ],
        out_specs=[
            pl.BlockSpec((tm, tn), lambda n, m: (m, n)),
            pl.BlockSpec((1, SUBLANES, tn),
                         lambda n, m: (jnp.minimum(m, npb - 1) // bps, 0, n)),
            pl.BlockSpec((tm, tn), sample_block),
        ],
        out_shape=[
            jax.ShapeDtypeStruct((m_all, d), BF16),
            jax.ShapeDtypeStruct((n_seq, SUBLANES, d), F32),
            jax.ShapeDtypeStruct((ms, d), F32),
        ],
        scratch_shapes=[pltpu.VMEM((SUBLANES, tn), F32),
                        pltpu.VMEM((3, d, tn), BF16)],
        compiler_params=_params(2),
        name="conv_in",
    )(h, w_in, w_in, w_in, conv_w, hist)


def _residual_tail(y, x, gpost, gnext):
    xn = x + _rms(y, gpost)
    return xn, _rms(xn, gnext)


def _load_weight_bf16(w_hbm, wbf_ref, stage_ref, sem, *, layer):
    chunk = stage_ref.shape[1]
    n_chunks = wbf_ref.shape[0] // chunk

    def copy(c):
        return pltpu.make_async_copy(
            w_hbm.at[layer, pl.ds(c * chunk, chunk), :],
            stage_ref.at[c % 2], sem.at[c % 2])

    copy(0).start()
    for c in range(n_chunks):
        if c + 1 < n_chunks:
            copy(c + 1).start()
        copy(c).wait()
        wbf_ref[pl.ds(c * chunk, chunk), :] = stage_ref[c % 2].astype(BF16)


def _weight_scratch(k, d):
    return [pltpu.VMEM((k, d), BF16),
            pltpu.VMEM((2, W_STAGE_ROWS, d), F32),
            pltpu.SemaphoreType.DMA((2,))]


def _proj_out_kernel(*refs, layer, n_prompt_blocks, split_res, final):
    m = pl.program_id(0)
    refs = list(refs)
    wbf_ref, stage_ref, sem = refs[-3:]
    refs = refs[:-3]
    a_ref, w_hbm = refs[:2]
    refs = refs[2:]

    @pl.when(m == 0)
    def _():
        _load_weight_bf16(w_hbm, wbf_ref, stage_ref, sem, layer=layer)

    if split_res:
        xp_ref, xs_ref = refs[:2]
        refs = refs[2:]
    else:
        x_ref = refs[0]
        refs = refs[1:]
    gpost_ref = refs[0]
    refs = refs[1:]
    if not final:
        gnext_ref = refs[0]
        refs = refs[1:]
    o0_ref, o1_ref = refs

    y = _dot(a_ref[...], wbf_ref[...])

    def finish(x):
        return _residual_tail(y, x, gpost_ref[...], gnext_ref[...])

    if split_res:
        @pl.when(m < n_prompt_blocks)
        def _():
            xn, hn = finish(xp_ref[...])
            o0_ref[...] = xn
            o1_ref[...] = hn.astype(o1_ref.dtype)

        @pl.when(m >= n_prompt_blocks)
        def _():
            xn, hn = finish(xs_ref[...])
            o0_ref[...] = xn
            o1_ref[...] = hn.astype(o1_ref.dtype)
    elif final:
        xn = x_ref[...] + _rms(y, gpost_ref[...])

        @pl.when(m < n_prompt_blocks)
        def _():
            o0_ref[...] = xn

        @pl.when(m >= n_prompt_blocks)
        def _():
            o1_ref[...] = xn
    else:
        xn, hn = finish(x_ref[...])
        o0_ref[...] = xn
        o1_ref[...] = hn.astype(o1_ref.dtype)


def _proj_out(a, w, res, gpost, gnext, *, layer, mp, tm):
    final = gnext is None
    m_all, k = a.shape
    d = w.shape[2]
    ms = m_all - mp
    npb, nsb = mp // tm, ms // tm
    split_res = isinstance(res, tuple)
    prompt_block = lambda m: (jnp.minimum(m, npb - 1), 0)
    sample_block = lambda m: (jnp.maximum(m - npb, 0), 0)
    row_block = lambda m: (m, 0)
    in_specs = [pl.BlockSpec((tm, k), row_block),
                pl.BlockSpec(memory_space=pl.ANY)]
    if split_res:
        in_specs += [pl.BlockSpec((tm, d), prompt_block),
                     pl.BlockSpec((tm, d), sample_block)]
        res_args = res
    else:
        in_specs += [pl.BlockSpec((tm, d), row_block)]
        res_args = (res,)
    norm_args = (gpost,) if final else (gpost, gnext)
    in_specs += [_resident((1, d), lambda m: (0, 0)) for _ in norm_args]
    if final:
        out_specs = [pl.BlockSpec((tm, d), prompt_block),
                     pl.BlockSpec((tm, d), sample_block)]
        out_shape = [jax.ShapeDtypeStruct((mp, d), F32),
                     jax.ShapeDtypeStruct((ms, d), F32)]
    else:
        out_specs = [pl.BlockSpec((tm, d), row_block),
                     pl.BlockSpec((tm, d), row_block)]
        out_shape = [jax.ShapeDtypeStruct((m_all, d), F32),
                     jax.ShapeDtypeStruct((m_all, d), BF16)]
    kernel = functools.partial(_proj_out_kernel, layer=layer, n_prompt_blocks=npb,
                               split_res=split_res, final=final)
    return pl.pallas_call(
        kernel,
        grid=(npb + nsb,),
        in_specs=in_specs,
        out_specs=out_specs,
        out_shape=out_shape,
        scratch_shapes=_weight_scratch(k, d),
        compiler_params=_params(1),
        name="proj_out_k%d" % k,
    )(a, w, *res_args, *norm_args)


def _ffn_up_kernel(h_ref, wg_ref, wu_ref, a_ref, wbf_ref):
    @pl.when(pl.program_id(1) == 0)
    def _():
        wbf_ref[0] = wg_ref[...].astype(BF16)
        wbf_ref[1] = wu_ref[...].astype(BF16)

    h = h_ref[...]
    g = _dot(h, wbf_ref[0])
    u = _dot(h, wbf_ref[1])
    a_ref[...] = (g * jax.nn.sigmoid(g) * u).astype(a_ref.dtype)


def _ffn_up(h, w_gate, w_up, *, layer, tm, tn):
    m_all, d = h.shape
    f = w_gate.shape[2]
    return pl.pallas_call(
        _ffn_up_kernel,
        grid=(f // tn, m_all // tm),
        in_specs=[
            pl.BlockSpec((tm, d), lambda n, m: (m, 0)),
            pl.BlockSpec((None, d, tn), lambda n, m: (layer, 0, n)),
            pl.BlockSpec((None, d, tn), lambda n, m: (layer, 0, n)),
        ],
        out_specs=pl.BlockSpec((tm, tn), lambda n, m: (m, n)),
        out_shape=jax.ShapeDtypeStruct((m_all, f), BF16),
        scratch_shapes=[pltpu.VMEM((2, d, tn), BF16)],
        compiler_params=_params(2),
        name="ffn_up",
    )(h, w_gate, w_up)


def _sgu_in_kernel(h_ref, w_ref, b_ref, zz_ref, wbf_ref):
    @pl.when(pl.program_id(1) == 0)
    def _():
        wbf_ref[...] = w_ref[...].astype(BF16)

    z = _dot(h_ref[...], wbf_ref[...]) + b_ref[...]
    gelu = 0.5 * z * (1.0 + lax.erf(z * math.sqrt(0.5)))
    zz_ref[...] = gelu.astype(zz_ref.dtype)


def _sgu_in(h, w, b, *, layer, tm, tn):
    m_all, d = h.shape
    n_out = w.shape[2]
    return pl.pallas_call(
        _sgu_in_kernel,
        grid=(n_out // tn, m_all // tm),
        in_specs=[
            pl.BlockSpec((tm, d), lambda n, m: (m, 0)),
            pl.BlockSpec((None, d, tn), lambda n, m: (layer, 0, n)),
            pl.BlockSpec((None, 1, tn), lambda n, m: (layer, 0, n)),
        ],
        out_specs=pl.BlockSpec((tm, tn), lambda n, m: (m, n)),
        out_shape=jax.ShapeDtypeStruct((m_all, n_out), BF16),
        scratch_shapes=[pltpu.VMEM((d, tn), BF16)],
        compiler_params=_params(2),
        name="sgu_in",
    )(h, w, b)


def _sgu_out_kernel(u_ref, v_ref, lng_ref, lnb_ref, wmix_ref, bmix_ref, w_hbm,
                    x_ref, gpost_ref, gnext_ref,
                    xo_ref, ho_ref, vn_ref, yin_ref, wbf_ref, stage_ref, sem,
                    *, layer, n_prompt_blocks, dec_seq):
    m = pl.program_id(0)

    @pl.when(m == 0)
    def _():
        _load_weight_bf16(w_hbm, wbf_ref, stage_ref, sem, layer=layer)

    tm, d = x_ref.shape
    n_groups = d // SGU_GROUP_DIM
    is_sample = m >= n_prompt_blocks
    kind = is_sample.astype(jnp.int32)

    v = v_ref[...].astype(F32)
    mu = jnp.mean(v, axis=-1, keepdims=True)
    vc = v - mu
    vn = vc * lax.rsqrt(jnp.mean(vc * vc, axis=-1, keepdims=True) + EPS)
    vn = vn * lng_ref[...] + lnb_ref[...]

    @pl.when(is_sample)
    def _():
        vn_ref[...] = vn

    vnb = vn.astype(BF16)

    i = lax.broadcasted_iota(jnp.int32, (SGU_CHUNK, SGU_CHUNK), 0)
    j = lax.broadcasted_iota(jnp.int32, (SGU_CHUNK, SGU_CHUNK), 1)
    same_seq = (i // dec_seq) == (j // dec_seq)
    mask = (i >= j) & (same_seq | jnp.logical_not(is_sample))
    bias = bmix_ref[kind]
    for g in range(n_groups):
        cols = slice(g * SGU_GROUP_DIM, (g + 1) * SGU_GROUP_DIM)
        wg = jnp.where(mask, wmix_ref[kind, g], 0.0).astype(BF16)
        for c in range(tm // SGU_CHUNK):
            rows = slice(c * SGU_CHUNK, (c + 1) * SGU_CHUNK)
            mixed = _dot(wg, vnb[rows, cols]) + bias[:, cols]
            u = u_ref[rows, cols].astype(F32)
            yin_ref[rows, cols] = (u * mixed).astype(yin_ref.dtype)

    y = _dot(yin_ref[...], wbf_ref[...])
    xn, hn = _residual_tail(y, x_ref[...], gpost_ref[...], gnext_ref[...])
    xo_ref[...] = xn
    ho_ref[...] = hn.astype(ho_ref.dtype)


def _sgu_out(zz, ln_g, ln_b, wmix, bmix, w_out, x, gpost, gnext,
             *, layer, mp, dec_seq, tm):
    m_all, d = x.shape
    ms = m_all - mp
    npb, nsb = mp // tm, ms // tm
    n_groups = d // SGU_GROUP_DIM
    row_block = lambda m: (m, 0)
    const2 = lambda m: (0, 0)
    kernel = functools.partial(_sgu_out_kernel, layer=layer,
                               n_prompt_blocks=npb, dec_seq=dec_seq)
    return pl.pallas_call(
        kernel,
        grid=(npb + nsb,),
        in_specs=[
            pl.BlockSpec((tm, d), lambda m: (m, 0)),
            pl.BlockSpec((tm, d), lambda m: (m, 1)),
            _resident((1, d), const2),
            _resident((1, d), const2),
            _resident((2, n_groups, SGU_CHUNK, SGU_CHUNK),
                      lambda m: (0, 0, 0, 0)),
            _resident((2, SGU_CHUNK, d), lambda m: (0, 0, 0)),
            pl.BlockSpec(memory_space=pl.ANY),
            pl.BlockSpec((tm, d), row_block),
            _resident((1, d), const2),
            _resident((1, d), const2),
        ],
        out_specs=[
            pl.BlockSpec((tm, d), row_block),
            pl.BlockSpec((tm, d), row_block),
            pl.BlockSpec((tm, d), lambda m: (jnp.maximum(m - npb, 0), 0)),
        ],
        out_shape=[
            jax.ShapeDtypeStruct((m_all, d), F32),
            jax.ShapeDtypeStruct((m_all, d), BF16),
            jax.ShapeDtypeStruct((ms, d), F32),
        ],
        scratch_shapes=[pltpu.VMEM((tm, d), BF16)] + _weight_scratch(d, d),
        compiler_params=_params(1),
        name="sgu_out",
    )(zz, zz, ln_g, ln_b, wmix, bmix, w_out, x, gpost, gnext)


TM_WIDE = 512
TM_FFN = 1088
TM_TAIL = 256
W_STAGE_ROWS = 512
TN_CONV = 512
TN_FFN = 512
TN_SGU = 1024


def kernel(x_prompt, x_sample, cache_conv, norm_mix_pre, norm_mix_post, norm_ffn_pre, norm_ffn_post, a_w_in, a_conv_w, a_w_out, b_w_in, b_b_in, b_ln_g, b_ln_b, b_w_s, b_b_s, b_w_out, ffn_w_gate, ffn_w_up, ffn_w_down):
    batch, seq, d = x_prompt.shape
    dec_batch, dec_seq, _ = x_sample.shape
    depth = norm_mix_pre.shape[0]
    mp, ms = batch * seq, dec_batch * dec_seq
    n_groups = d // SGU_GROUP_DIM
    assert depth == 2 and mp % TM_WIDE == 0 and ms % TM_WIDE == 0
    assert seq % TM_WIDE == 0 and SGU_CHUNK % dec_seq == 0 and dec_seq >= CONV_W - 1
    assert seq % SGU_CHUNK == 0 and TM_TAIL % SGU_CHUNK == 0
    assert (mp + ms) % TM_FFN == 0

    xp = x_prompt.reshape(mp, d)
    xs = x_sample.reshape(ms, d)
    row = lambda p, i: p[i].reshape(1, -1)

    h = _prenorm(xp, xs, row(norm_mix_pre, 0), tm=TM_WIDE)
    hist = jnp.pad(cache_conv[0], ((0, 0), (0, dec_seq - (CONV_W - 1)), (0, 0)))
    p, tails, cz_s = _conv_in(h, a_w_in, a_conv_w[0], hist.reshape(ms, d),
                              layer=0, mp=mp, seq=seq, dec_seq=dec_seq,
                              tm=TM_WIDE, tn=TN_CONV)
    state_conv_prompt = tails[None, :, SUBLANES - (CONV_W - 1):, :]
    state_conv_sample = cz_s.reshape(dec_batch, dec_seq, d)[None, :, dec_seq - (CONV_W - 1):, :]
    x, h = _proj_out(p, a_w_out, (xp, xs), row(norm_mix_post, 0),
                     row(norm_ffn_pre, 0), layer=0, mp=mp, tm=TM_TAIL)

    a = _ffn_up(h, ffn_w_gate, ffn_w_up, layer=0, tm=TM_FFN, tn=TN_FFN)
    x, h = _proj_out(a, ffn_w_down, x, row(norm_ffn_post, 0),
                     row(norm_mix_pre, 1), layer=0, mp=mp, tm=TM_TAIL)

    zz = _sgu_in(h, b_w_in, b_b_in[:, None, :], layer=0, tm=TM_WIDE, tn=TN_SGU)
    reps = SGU_CHUNK // dec_seq
    w_s = b_w_s[0]
    wmix = jnp.stack([w_s, jnp.tile(w_s[:, :dec_seq, :dec_seq], (1, reps, reps))])
    b_s = b_b_s[0]
    b_rows = jnp.stack([b_s, jnp.tile(b_s[:, :dec_seq], (1, reps))])
    bmix = jnp.repeat(jnp.swapaxes(b_rows, 1, 2), SGU_GROUP_DIM, axis=2)
    x, h, vn_s = _sgu_out(zz, b_ln_g[0].reshape(1, -1), b_ln_b[0].reshape(1, -1),
                          wmix, bmix, b_w_out, x,
                          row(norm_mix_post, 1), row(norm_ffn_pre, 1),
                          layer=0, mp=mp, dec_seq=dec_seq, tm=TM_TAIL)
    state_sgu_v_sample = vn_s.reshape(1, dec_batch, dec_seq, d)

    a = _ffn_up(h, ffn_w_gate, ffn_w_up, layer=1, tm=TM_FFN, tn=TN_FFN)
    yp, ys = _proj_out(a, ffn_w_down, x, row(norm_ffn_post, 1),
                       None, layer=1, mp=mp, tm=TM_TAIL)

    return (yp.reshape(batch, seq, d), ys.reshape(dec_batch, dec_seq, d),
            state_conv_prompt, state_conv_sample, state_sgu_v_sample)
```

```python
import functools
import math

import jax
import jax.numpy as jnp
from jax import lax
from jax.experimental import pallas as pl
from jax.experimental.pallas import tpu as pltpu

EPS = 1e-6
CONV_W = 3
SGU_CHUNK = 128
SGU_GROUP_DIM = 128
SUBLANES = 8

MIB = 1024 * 1024
VMEM_LIMIT_BYTES = 56 * MIB

F32 = jnp.float32
BF16 = jnp.bfloat16


def _params(n_axes):
    return pltpu.CompilerParams(
        dimension_semantics=("arbitrary",) * n_axes,
        vmem_limit_bytes=VMEM_LIMIT_BYTES)


def _rms(x, g):
    y = x * lax.rsqrt(jnp.mean(x * x, axis=-1, keepdims=True) + EPS)
    return y * g


def _dot(a, b):
    return jnp.dot(a, b, preferred_element_type=F32)


def _resident(block_shape, index_map):
    return pl.BlockSpec(block_shape, index_map, pipeline_mode=pl.Buffered(1))


def _prenorm_kernel(xp_ref, xs_ref, g_ref, h_ref, *, n_prompt_blocks):
    m = pl.program_id(0)

    @pl.when(m < n_prompt_blocks)
    def _():
        h_ref[...] = _rms(xp_ref[...], g_ref[...]).astype(h_ref.dtype)

    @pl.when(m >= n_prompt_blocks)
    def _():
        h_ref[...] = _rms(xs_ref[...], g_ref[...]).astype(h_ref.dtype)


def _prenorm(xp, xs, g, *, tm):
    mp, d = xp.shape
    ms = xs.shape[0]
    npb, nsb = mp // tm, ms // tm
    return pl.pallas_call(
        functools.partial(_prenorm_kernel, n_prompt_blocks=npb),
        grid=(npb + nsb,),
        in_specs=[
            pl.BlockSpec((tm, d), lambda m: (jnp.minimum(m, npb - 1), 0)),
            pl.BlockSpec((tm, d), lambda m: (jnp.maximum(m - npb, 0), 0)),
            _resident((1, d), lambda m: (0, 0)),
        ],
        out_specs=pl.BlockSpec((tm, d), lambda m: (m, 0)),
        out_shape=jax.ShapeDtypeStruct((mp + ms, d), BF16),
        compiler_params=_params(1),
        name="prenorm",
    )(xp, xs, g)


def _conv_in_kernel(h_ref, wb_ref, wc_ref, wz_ref, cw_ref, hist_ref,
                    p_ref, tail_ref, czs_ref, carry_ref, wbf_ref,
                    *, n_prompt_blocks, blocks_per_seq, dec_seq, sub):
    m = pl.program_id(1)
    tm, tn = p_ref.shape

    @pl.when(m == 0)
    def _():
        wbf_ref[0] = wb_ref[...].astype(BF16)
        wbf_ref[1] = wc_ref[...].astype(BF16)
        wbf_ref[2] = wz_ref[...].astype(BF16)

    @pl.when(m % blocks_per_seq == 0)
    def _():
        carry_ref[...] = jnp.zeros_like(carry_ref)

    is_sample = m >= n_prompt_blocks
    cw = cw_ref[...]
    row = lax.broadcasted_iota(jnp.int32, (sub, tn), 0)
    pos = row % dec_seq
    prev = carry_ref[...]
    for s in range(tm // sub):
        rows = pl.ds(s * sub, sub)
        h = h_ref[rows, :]
        gate_b = _dot(h, wbf_ref[0])
        cz = _dot(h, wbf_ref[1]) * _dot(h, wbf_ref[2])

        prev1 = prev[SUBLANES - 1:SUBLANES, :]
        prev2 = prev[SUBLANES - 2:SUBLANES - 1, :]
        cz1 = jnp.where(row == 0, prev1, pltpu.roll(cz, 1, 0))
        cz2 = jnp.where(row == 0, prev2,
                        jnp.where(row == 1, prev1, pltpu.roll(cz, 2, 0)))

        hist = hist_ref[rows, :]
        cz1 = jnp.where(is_sample & (pos == 0), pltpu.roll(hist, sub - 1, 0), cz1)
        cz2 = jnp.where(is_sample & (pos < CONV_W - 1), hist, cz2)

        conv = cw[0:1, :] * cz2 + cw[1:2, :] * cz1 + cw[2:3, :] * cz
        p_ref[rows, :] = (gate_b * conv).astype(p_ref.dtype)
        czs_ref[rows, :] = cz
        prev = cz[sub - SUBLANES:, :]

    carry_ref[...] = prev
    tail_ref[0] = prev


def _conv_in(h, w_in, conv_w, hist, *, layer, mp, seq, dec_seq, tm, tn, sub):
    m_all, d = h.shape
    ms = m_all - mp
    npb, nsb, bps = mp // tm, ms // tm, seq // tm
    n_seq = mp // seq
    nt = d // tn
    kernel = functools.partial(_conv_in_kernel, n_prompt_blocks=npb,
                               blocks_per_seq=bps, dec_seq=dec_seq, sub=sub)
    sample_block = lambda n, m: (jnp.maximum(m - npb, 0), n)
    return pl.pallas_call(
        kernel,
        grid=(nt, npb + nsb),
        in_specs=[
            pl.BlockSpec((tm, d), lambda n, m: (m, 0)),
            pl.BlockSpec((None, d, tn), lambda n, m: (layer, 0, n)),
            pl.BlockSpec((None, d, tn), lambda n, m: (layer, 0, nt + n)),
            pl.BlockSpec((None, d, tn), lambda n, m: (layer, 0, 2 * nt + n)),
            pl.BlockSpec((CONV_W, tn), lambda n, m: (0, n)),
            pl.BlockSpec((tm, tn), sample_block),
        ],
        out_specs=[
            pl.BlockSpec((tm, tn), lambda n, m: (m, n)),
            pl.BlockSpec((1, SUBLANES, tn),
                         lambda n, m: (jnp.minimum(m // bps, n_seq), 0, n)),
            pl.BlockSpec((tm, tn), sample_block),
        ],
        out_shape=[
            jax.ShapeDtypeStruct((m_all, d), BF16),
            jax.ShapeDtypeStruct((n_seq + 1, SUBLANES, d), F32),
            jax.ShapeDtypeStruct((ms, d), F32),
        ],
        scratch_shapes=[pltpu.VMEM((SUBLANES, tn), F32),
                        pltpu.VMEM((3, d, tn), BF16)],
        compiler_params=_params(2),
        name="conv_in",
    )(h, w_in, w_in, w_in, conv_w, hist)


def _load_weight_bf16(w_hbm, wbf_ref, stage_ref, sem, *, layer):
    chunk = stage_ref.shape[1]
    n_chunks = wbf_ref.shape[0] // chunk

    def copy(c):
        return pltpu.make_async_copy(
            w_hbm.at[layer, pl.ds(c * chunk, chunk), :],
            stage_ref.at[c % 2], sem.at[c % 2])

    copy(0).start()
    for c in range(n_chunks):
        if c + 1 < n_chunks:
            copy(c + 1).start()
        copy(c).wait()
        wbf_ref[pl.ds(c * chunk, chunk), :] = stage_ref[c % 2].astype(BF16)


def _weight_scratch(k, d):
    return [pltpu.VMEM((k, d), BF16),
            pltpu.VMEM((2, W_STAGE_ROWS, d), F32),
            pltpu.SemaphoreType.DMA((2,))]


def _proj_out_kernel(*refs, layer, n_prompt_blocks, split_res, final, sub):
    m = pl.program_id(0)
    refs = list(refs)
    wbf_ref, stage_ref, sem = refs[-3:]
    refs = refs[:-3]
    a_ref, w_hbm = refs[:2]
    refs = refs[2:]

    @pl.when(m == 0)
    def _():
        _load_weight_bf16(w_hbm, wbf_ref, stage_ref, sem, layer=layer)

    if split_res:
        xp_ref, xs_ref = refs[:2]
        refs = refs[2:]
    else:
        x_ref = refs[0]
        refs = refs[1:]
    gpost_ref = refs[0]
    refs = refs[1:]
    if not final:
        gnext_ref = refs[0]
        refs = refs[1:]
    o0_ref, o1_ref = refs

    is_prompt = m < n_prompt_blocks
    tm = a_ref.shape[0]

    def run(x_ref, xo_ref, ho_ref):
        for s in range(tm // sub):
            rows = pl.ds(s * sub, sub)
            y = _dot(a_ref[rows, :], wbf_ref[...])
            xn = x_ref[rows, :] + _rms(y, gpost_ref[...])
            xo_ref[rows, :] = xn
            if ho_ref is not None:
                ho_ref[rows, :] = _rms(xn, gnext_ref[...]).astype(ho_ref.dtype)

    if split_res:
        pl.when(is_prompt)(lambda: run(xp_ref, o0_ref, o1_ref))
        pl.when(jnp.logical_not(is_prompt))(lambda: run(xs_ref, o0_ref, o1_ref))
    elif final:
        pl.when(is_prompt)(lambda: run(x_ref, o0_ref, None))
        pl.when(jnp.logical_not(is_prompt))(lambda: run(x_ref, o1_ref, None))
    else:
        run(x_ref, o0_ref, o1_ref)


def _proj_out(a, w, res, gpost, gnext, *, layer, mp, tm, sub):
    final = gnext is None
    m_all, k = a.shape
    d = w.shape[2]
    ms = m_all - mp
    npb, nsb = mp // tm, ms // tm
    split_res = isinstance(res, tuple)
    prompt_block = lambda m: (jnp.minimum(m, npb - 1), 0)
    sample_block = lambda m: (jnp.maximum(m - npb, 0), 0)
    row_block = lambda m: (m, 0)
    in_specs = [pl.BlockSpec((tm, k), row_block),
                pl.BlockSpec(memory_space=pl.ANY)]
    if split_res:
        in_specs += [pl.BlockSpec((tm, d), prompt_block),
                     pl.BlockSpec((tm, d), sample_block)]
        res_args = res
    else:
        in_specs += [pl.BlockSpec((tm, d), row_block)]
        res_args = (res,)
    norm_args = (gpost,) if final else (gpost, gnext)
    in_specs += [_resident((1, d), lambda m: (0, 0)) for _ in norm_args]
    if final:
        out_specs = [pl.BlockSpec((tm, d), prompt_block),
                     pl.BlockSpec((tm, d), sample_block)]
        out_shape = [jax.ShapeDtypeStruct((mp, d), F32),
                     jax.ShapeDtypeStruct((ms, d), F32)]
    else:
        out_specs = [pl.BlockSpec((tm, d), row_block),
                     pl.BlockSpec((tm, d), row_block)]
        out_shape = [jax.ShapeDtypeStruct((m_all, d), F32),
                     jax.ShapeDtypeStruct((m_all, d), BF16)]
    kernel = functools.partial(_proj_out_kernel, layer=layer, n_prompt_blocks=npb,
                               split_res=split_res, final=final, sub=sub)
    return pl.pallas_call(
        kernel,
        grid=(npb + nsb,),
        in_specs=in_specs,
        out_specs=out_specs,
        out_shape=out_shape,
        scratch_shapes=_weight_scratch(k, d),
        compiler_params=_params(1),
        name="proj_out_k%d" % k,
    )(a, w, *res_args, *norm_args)


def _ffn_up_kernel(h_ref, wg_ref, wu_ref, a_ref, wbf_ref, *, sub):
    @pl.when(pl.program_id(1) == 0)
    def _():
        wbf_ref[0] = wg_ref[...].astype(BF16)
        wbf_ref[1] = wu_ref[...].astype(BF16)

    for s in range(a_ref.shape[0] // sub):
        rows = pl.ds(s * sub, sub)
        h = h_ref[rows, :]
        g = _dot(h, wbf_ref[0])
        u = _dot(h, wbf_ref[1])
        a_ref[rows, :] = (g * jax.nn.sigmoid(g) * u).astype(a_ref.dtype)


def _ffn_up(h, w_gate, w_up, *, layer, tm, tn, sub):
    m_all, d = h.shape
    f = w_gate.shape[2]
    return pl.pallas_call(
        functools.partial(_ffn_up_kernel, sub=sub),
        grid=(f // tn, m_all // tm),
        in_specs=[
            pl.BlockSpec((tm, d), lambda n, m: (m, 0)),
            pl.BlockSpec((None, d, tn), lambda n, m: (layer, 0, n)),
            pl.BlockSpec((None, d, tn), lambda n, m: (layer, 0, n)),
        ],
        out_specs=pl.BlockSpec((tm, tn), lambda n, m: (m, n)),
        out_shape=jax.ShapeDtypeStruct((m_all, f), BF16),
        scratch_shapes=[pltpu.VMEM((2, d, tn), BF16)],
        compiler_params=_params(2),
        name="ffn_up",
    )(h, w_gate, w_up)


def _sgu_in_kernel(h_ref, w_ref, b_ref, zz_ref, wbf_ref, *, sub):
    @pl.when(pl.program_id(1) == 0)
    def _():
        wbf_ref[...] = w_ref[...].astype(BF16)

    for s in range(zz_ref.shape[0] // sub):
        rows = pl.ds(s * sub, sub)
        z = _dot(h_ref[rows, :], wbf_ref[...]) + b_ref[...]
        gelu = 0.5 * z * (1.0 + lax.erf(z * math.sqrt(0.5)))
        zz_ref[rows, :] = gelu.astype(zz_ref.dtype)


def _sgu_in(h, w, b, *, layer, tm, tn, sub):
    m_all, d = h.shape
    n_out = w.shape[2]
    return pl.pallas_call(
        functools.partial(_sgu_in_kernel, sub=sub),
        grid=(n_out // tn, m_all // tm),
        in_specs=[
            pl.BlockSpec((tm, d), lambda n, m: (m, 0)),
            pl.BlockSpec((None, d, tn), lambda n, m: (layer, 0, n)),
            pl.BlockSpec((None, 1, tn), lambda n, m: (layer, 0, n)),
        ],
        out_specs=pl.BlockSpec((tm, tn), lambda n, m: (m, n)),
        out_shape=jax.ShapeDtypeStruct((m_all, n_out), BF16),
        scratch_shapes=[pltpu.VMEM((d, tn), BF16)],
        compiler_params=_params(2),
        name="sgu_in",
    )(h, w, b)


def _sgu_out_kernel(u_ref, v_ref, lng_ref, lnb_ref, wmix_ref, bmix_ref, w_hbm,
                    x_ref, gpost_ref, gnext_ref,
                    xo_ref, ho_ref, vn_ref,
                    mixw_ref, wbf_ref, stage_ref, sem,
                    *, layer, n_prompt_blocks, dec_seq):
    m = pl.program_id(0)
    tm, d = x_ref.shape
    n_groups = d // SGU_GROUP_DIM
    is_sample = m >= n_prompt_blocks
    kind = is_sample.astype(jnp.int32)

    @pl.when(m == 0)
    def _():
        _load_weight_bf16(w_hbm, wbf_ref, stage_ref, sem, layer=layer)

    @pl.when((m == 0) | (m == n_prompt_blocks))
    def _():
        i = lax.broadcasted_iota(jnp.int32, (SGU_CHUNK, SGU_CHUNK), 0)
        j = lax.broadcasted_iota(jnp.int32, (SGU_CHUNK, SGU_CHUNK), 1)
        same_seq = (i // dec_seq) == (j // dec_seq)
        mask = (i >= j) & (same_seq | jnp.logical_not(is_sample))
        for g in range(n_groups):
            mixw_ref[g] = jnp.where(mask, wmix_ref[kind, g], 0.0).astype(BF16)

    for c in range(tm // SGU_CHUNK):
        rows = pl.ds(c * SGU_CHUNK, SGU_CHUNK)
        v = v_ref[rows, :].astype(F32)
        mu = jnp.mean(v, axis=-1, keepdims=True)
        vc = v - mu
        vn = vc * lax.rsqrt(jnp.mean(vc * vc, axis=-1, keepdims=True) + EPS)
        vn = vn * lng_ref[...] + lnb_ref[...]
        vn_ref[rows, :] = vn
        vnb = vn.astype(BF16)

        gated = []
        for g in range(n_groups):
            cols = slice(g * SGU_GROUP_DIM, (g + 1) * SGU_GROUP_DIM)
            mixed = _dot(mixw_ref[g], vnb[:, cols]) + bmix_ref[kind, :, cols]
            gated.append((u_ref[rows, cols].astype(F32) * mixed).astype(BF16))
        y = _dot(jnp.concatenate(gated, axis=1), wbf_ref[...])

        xn = x_ref[rows, :] + _rms(y, gpost_ref[...])
        xo_ref[rows, :] = xn
        ho_ref[rows, :] = _rms(xn, gnext_ref[...]).astype(ho_ref.dtype)


def _sgu_out(zz, ln_g, ln_b, wmix, bmix, w_out, x, gpost, gnext,
             *, layer, mp, dec_seq, tm):
    m_all, d = x.shape
    ms = m_all - mp
    npb, nsb = mp // tm, ms // tm
    n_groups = d // SGU_GROUP_DIM
    row_block = lambda m: (m, 0)
    const2 = lambda m: (0, 0)
    kernel = functools.partial(_sgu_out_kernel, layer=layer,
                               n_prompt_blocks=npb, dec_seq=dec_seq)
    return pl.pallas_call(
        kernel,
        grid=(npb + nsb,),
        in_specs=[
            pl.BlockSpec((tm, d), lambda m: (m, 0)),
            pl.BlockSpec((tm, d), lambda m: (m, 1)),
            _resident((1, d), const2),
            _resident((1, d), const2),
            _resident((2, n_groups, SGU_CHUNK, SGU_CHUNK),
                      lambda m: (0, 0, 0, 0)),
            _resident((2, SGU_CHUNK, d), lambda m: (0, 0, 0)),
            pl.BlockSpec(memory_space=pl.ANY),
            pl.BlockSpec((tm, d), row_block),
            _resident((1, d), const2),
            _resident((1, d), const2),
        ],
        out_specs=[
            pl.BlockSpec((tm, d), row_block),
            pl.BlockSpec((tm, d), row_block),
            pl.BlockSpec((tm, d), lambda m: (jnp.maximum(m - npb, 0), 0)),
        ],
        out_shape=[
            jax.ShapeDtypeStruct((m_all, d), F32),
            jax.ShapeDtypeStruct((m_all, d), BF16),
            jax.ShapeDtypeStruct((ms, d), F32),
        ],
        scratch_shapes=[pltpu.VMEM((n_groups, SGU_CHUNK, SGU_CHUNK), BF16)]
        + _weight_scratch(d, d),
        compiler_params=_params(1),
        name="sgu_out",
    )(zz, zz, ln_g, ln_b, wmix, bmix, w_out, x, gpost, gnext)


TM_WIDE = 512
SUB_WIDE = 256
TM_FFN = 2176
SUB_FFN = 272
TM_TAIL = 256
SUB_TAIL = 128
W_STAGE_ROWS = 512
TN_CONV = 512
TN_FFN = 512
TN_SGU = 1024


def kernel(x_prompt, x_sample, cache_conv, norm_mix_pre, norm_mix_post, norm_ffn_pre, norm_ffn_post, a_w_in, a_conv_w, a_w_out, b_w_in, b_b_in, b_ln_g, b_ln_b, b_w_s, b_b_s, b_w_out, ffn_w_gate, ffn_w_up, ffn_w_down):
    batch, seq, d = x_prompt.shape
    dec_batch, dec_seq, _ = x_sample.shape
    depth = norm_mix_pre.shape[0]
    mp, ms = batch * seq, dec_batch * dec_seq
    assert depth == 2 and mp % TM_WIDE == 0 and ms % TM_WIDE == 0
    assert seq % TM_WIDE == 0 and SGU_CHUNK % dec_seq == 0 and dec_seq >= CONV_W - 1
    assert seq % SGU_CHUNK == 0 and TM_TAIL % SGU_CHUNK == 0
    assert (mp + ms) % TM_FFN == 0 and SUB_WIDE % dec_seq == 0

    xp = x_prompt.reshape(mp, d)
    xs = x_sample.reshape(ms, d)
    row = lambda p, i: p[i].reshape(1, -1)

    h = _prenorm(xp, xs, row(norm_mix_pre, 0), tm=TM_WIDE)
    hist = jnp.pad(cache_conv[0], ((0, 0), (0, dec_seq - (CONV_W - 1)), (0, 0)))
    p, tails, cz_s = _conv_in(h, a_w_in, a_conv_w[0], hist.reshape(ms, d),
                              layer=0, mp=mp, seq=seq, dec_seq=dec_seq,
                              tm=TM_WIDE, tn=TN_CONV, sub=SUB_WIDE)
    state_conv_prompt = tails[None, :batch, SUBLANES - (CONV_W - 1):, :]
    state_conv_sample = cz_s.reshape(dec_batch, dec_seq, d)[None, :, dec_seq - (CONV_W - 1):, :]
    x, h = _proj_out(p, a_w_out, (xp, xs), row(norm_mix_post, 0),
                     row(norm_ffn_pre, 0), layer=0, mp=mp, tm=TM_TAIL, sub=SUB_TAIL)

    a = _ffn_up(h, ffn_w_gate, ffn_w_up, layer=0, tm=TM_FFN, tn=TN_FFN, sub=SUB_FFN)
    x, h = _proj_out(a, ffn_w_down, x, row(norm_ffn_post, 0),
                     row(norm_mix_pre, 1), layer=0, mp=mp, tm=TM_TAIL, sub=SUB_TAIL)

    zz = _sgu_in(h, b_w_in, b_b_in[:, None, :], layer=0, tm=TM_FFN, tn=TN_SGU,
                 sub=SUB_FFN)
    reps = SGU_CHUNK // dec_seq
    w_s = b_w_s[0]
    wmix = jnp.stack([w_s, jnp.tile(w_s[:, :dec_seq, :dec_seq], (1, reps, reps))])
    b_s = b_b_s[0]
    b_rows = jnp.stack([b_s, jnp.tile(b_s[:, :dec_seq], (1, reps))])
    bmix = jnp.repeat(jnp.swapaxes(b_rows, 1, 2), SGU_GROUP_DIM, axis=2)
    x, h, vn_s = _sgu_out(zz, b_ln_g[0].reshape(1, -1), b_ln_b[0].reshape(1, -1),
                          wmix, bmix, b_w_out, x,
                          row(norm_mix_post, 1), row(norm_ffn_pre, 1),
                          layer=0, mp=mp, dec_seq=dec_seq, tm=TM_TAIL)
    state_sgu_v_sample = vn_s.reshape(1, dec_batch, dec_seq, d)

    a = _ffn_up(h, ffn_w_gate, ffn_w_up, layer=1, tm=TM_FFN, tn=TN_FFN, sub=SUB_FFN)
    yp, ys = _proj_out(a, ffn_w_down, x, row(norm_ffn_post, 1),
                       None, layer=1, mp=mp, tm=TM_TAIL, sub=SUB_TAIL)

    return (yp.reshape(batch, seq, d), ys.reshape(dec_batch, dec_seq, d),
            state_conv_prompt, state_conv_sample, state_sgu_v_sample)
```

```python
import functools
import math

import jax
import jax.numpy as jnp
from jax import lax
from jax.experimental import pallas as pl
from jax.experimental.pallas import tpu as pltpu

EPS = 1e-6
CONV_W = 3
SGU_CHUNK = 128
SGU_GROUP_DIM = 128
SUBLANES = 8

MIB = 1024 * 1024
VMEM_LIMIT_BYTES = 56 * MIB

F32 = jnp.float32
BF16 = jnp.bfloat16


def _params(n_axes):
    return pltpu.CompilerParams(
        dimension_semantics=("arbitrary",) * n_axes,
        vmem_limit_bytes=VMEM_LIMIT_BYTES)


def _rms(x, g):
    y = x * lax.rsqrt(jnp.mean(x * x, axis=-1, keepdims=True) + EPS)
    return y * g


def _dot(a, b):
    return jnp.dot(a, b, preferred_element_type=F32)


def _resident(block_shape, index_map):
    return pl.BlockSpec(block_shape, index_map, pipeline_mode=pl.Buffered(1))


def _conv_in_kernel(*refs, blocks_per_seq, dec_seq, sub, sample):
    if sample:
        (x_ref, g_ref, wb_ref, wc_ref, wz_ref, cw_ref, hist_ref,
         p_ref, state_ref, wbf_ref) = refs
    else:
        (x_ref, g_ref, wb_ref, wc_ref, wz_ref, cw_ref,
         p_ref, state_ref, wbf_ref, carry_ref) = refs
    m = pl.program_id(1)
    tm, tn = p_ref.shape

    @pl.when(m == 0)
    def _():
        wbf_ref[0] = wb_ref[...].astype(BF16)
        wbf_ref[1] = wc_ref[...].astype(BF16)
        wbf_ref[2] = wz_ref[...].astype(BF16)

    if not sample:
        @pl.when(m % blocks_per_seq == 0)
        def _():
            carry_ref[...] = jnp.zeros_like(carry_ref)

        prev = carry_ref[...]

    g = g_ref[...]
    cw = cw_ref[...]
    row = lax.broadcasted_iota(jnp.int32, (sub, tn), 0)
    for s in range(tm // sub):
        rows = pl.ds(s * sub, sub)
        h = _rms(x_ref[rows, :], g).astype(BF16)
        gate_b = _dot(h, wbf_ref[0])
        cz = _dot(h, wbf_ref[1]) * _dot(h, wbf_ref[2])

        if sample:
            hist = hist_ref[rows, :]
            pos = row % dec_seq
            cz1 = jnp.where(pos == 0, pltpu.roll(hist, sub - 1, 0), pltpu.roll(cz, 1, 0))
            cz2 = jnp.where(pos < CONV_W - 1, hist, pltpu.roll(cz, 2, 0))
            state_ref[rows, :] = cz
        else:
            prev1 = prev[SUBLANES - 1:SUBLANES, :]
            prev2 = prev[SUBLANES - 2:SUBLANES - 1, :]
            cz1 = jnp.where(row == 0, prev1, pltpu.roll(cz, 1, 0))
            cz2 = jnp.where(row == 0, prev2,
                            jnp.where(row == 1, prev1, pltpu.roll(cz, 2, 0)))
            prev = cz[sub - SUBLANES:, :]

        conv = cw[0:1, :] * cz2 + cw[1:2, :] * cz1 + cw[2:3, :] * cz
        p_ref[rows, :] = (gate_b * conv).astype(p_ref.dtype)

    if not sample:
        carry_ref[...] = prev
        state_ref[0] = prev


def _conv_in(x, g, w_in, conv_w, hist, *, layer, seq, dec_seq, tm, tn, sub):
    sample = hist is not None
    rows, d = x.shape
    nt = d // tn
    bps = seq // tm
    kernel = functools.partial(_conv_in_kernel, blocks_per_seq=bps,
                               dec_seq=dec_seq, sub=sub, sample=sample)
    row_tile = pl.BlockSpec((tm, tn), lambda n, m: (m, n))
    in_specs = [
        pl.BlockSpec((tm, d), lambda n, m: (m, 0)),
        pl.BlockSpec((1, d), lambda n, m: (0, 0)),
        pl.BlockSpec((None, d, tn), lambda n, m: (layer, 0, n)),
        pl.BlockSpec((None, d, tn), lambda n, m: (layer, 0, nt + n)),
        pl.BlockSpec((None, d, tn), lambda n, m: (layer, 0, 2 * nt + n)),
        pl.BlockSpec((CONV_W, tn), lambda n, m: (0, n)),
    ]
    scratch = [pltpu.VMEM((3, d, tn), BF16)]
    if sample:
        in_specs.append(row_tile)
        args = (x, g, w_in, w_in, w_in, conv_w, hist)
        state_spec = row_tile
        state_shape = jax.ShapeDtypeStruct((rows, d), F32)
    else:
        args = (x, g, w_in, w_in, w_in, conv_w)
        state_spec = pl.BlockSpec((1, SUBLANES, tn), lambda n, m: (m // bps, 0, n))
        state_shape = jax.ShapeDtypeStruct((rows // seq, SUBLANES, d), F32)
        scratch.append(pltpu.VMEM((SUBLANES, tn), F32))
    return pl.pallas_call(
        kernel,
        grid=(nt, rows // tm),
        in_specs=in_specs,
        out_specs=[row_tile, state_spec],
        out_shape=[jax.ShapeDtypeStruct((rows, d), BF16), state_shape],
        scratch_shapes=scratch,
        compiler_params=_params(2),
        name="conv_in_sample" if sample else "conv_in_prompt",
    )(*args)


def _load_weight_bf16(w_hbm, wbf_ref, stage_ref, sem, *, layer):
    chunk = stage_ref.shape[1]
    n_chunks = wbf_ref.shape[0] // chunk

    def copy(c):
        return pltpu.make_async_copy(
            w_hbm.at[layer, pl.ds(c * chunk, chunk), :],
            stage_ref.at[c % 2], sem.at[c % 2])

    copy(0).start()
    for c in range(n_chunks):
        if c + 1 < n_chunks:
            copy(c + 1).start()
        copy(c).wait()
        wbf_ref[pl.ds(c * chunk, chunk), :] = stage_ref[c % 2].astype(BF16)


def _weight_scratch(k, d):
    return [pltpu.VMEM((k, d), BF16),
            pltpu.VMEM((2, W_STAGE_ROWS, d), F32),
            pltpu.SemaphoreType.DMA((2,))]


def _proj_out_kernel(*refs, layer, n_prompt_blocks, split_in, final, sub):
    m = pl.program_id(0)
    refs = list(refs)
    wbf_ref, stage_ref, sem = refs[-3:]
    refs = refs[:-3]
    if split_in:
        ap_ref, as_ref, w_hbm, xp_ref, xs_ref = refs[:5]
        refs = refs[5:]
    else:
        a_ref, w_hbm, x_ref = refs[:3]
        refs = refs[3:]
    gpost_ref = refs[0]
    refs = refs[1:]
    if not final:
        gnext_ref = refs[0]
        refs = refs[1:]
    o0_ref, o1_ref = refs

    @pl.when(m == 0)
    def _():
        _load_weight_bf16(w_hbm, wbf_ref, stage_ref, sem, layer=layer)

    is_prompt = m < n_prompt_blocks

    def run(a_ref, x_ref, xo_ref, ho_ref):
        for s in range(a_ref.shape[0] // sub):
            rows = pl.ds(s * sub, sub)
            y = _dot(a_ref[rows, :], wbf_ref[...])
            xn = x_ref[rows, :] + _rms(y, gpost_ref[...])
            xo_ref[rows, :] = xn
            if ho_ref is not None:
                ho_ref[rows, :] = _rms(xn, gnext_ref[...]).astype(ho_ref.dtype)

    if split_in:
        pl.when(is_prompt)(lambda: run(ap_ref, xp_ref, o0_ref, o1_ref))
        pl.when(jnp.logical_not(is_prompt))(lambda: run(as_ref, xs_ref, o0_ref, o1_ref))
    elif final:
        pl.when(is_prompt)(lambda: run(a_ref, x_ref, o0_ref, None))
        pl.when(jnp.logical_not(is_prompt))(lambda: run(a_ref, x_ref, o1_ref, None))
    else:
        run(a_ref, x_ref, o0_ref, o1_ref)


def _proj_out(a, w, res, gpost, gnext, *, layer, mp, tm, sub):
    final = gnext is None
    split_in = isinstance(a, tuple)
    assert split_in == isinstance(res, tuple)
    k, d = w.shape[1:]
    m_all = sum(t.shape[0] for t in a) if split_in else a.shape[0]
    ms = m_all - mp
    npb, nsb = mp // tm, ms // tm
    prompt_block = lambda m: (jnp.minimum(m, npb - 1), 0)
    sample_block = lambda m: (jnp.maximum(m - npb, 0), 0)
    row_block = lambda m: (m, 0)
    w_spec = pl.BlockSpec(memory_space=pl.ANY)
    if split_in:
        in_specs = [pl.BlockSpec((tm, k), prompt_block), pl.BlockSpec((tm, k), sample_block),
                    w_spec,
                    pl.BlockSpec((tm, d), prompt_block), pl.BlockSpec((tm, d), sample_block)]
        args = (*a, w, *res)
    else:
        in_specs = [pl.BlockSpec((tm, k), row_block), w_spec,
                    pl.BlockSpec((tm, d), row_block)]
        args = (a, w, res)
    norm_args = (gpost,) if final else (gpost, gnext)
    in_specs += [_resident((1, d), lambda m: (0, 0)) for _ in norm_args]
    if final:
        out_specs = [pl.BlockSpec((tm, d), prompt_block),
                     pl.BlockSpec((tm, d), sample_block)]
        out_shape = [jax.ShapeDtypeStruct((mp, d), F32),
                     jax.ShapeDtypeStruct((ms, d), F32)]
    else:
        out_specs = [pl.BlockSpec((tm, d), row_block),
                     pl.BlockSpec((tm, d), row_block)]
        out_shape = [jax.ShapeDtypeStruct((m_all, d), F32),
                     jax.ShapeDtypeStruct((m_all, d), BF16)]
    kernel = functools.partial(_proj_out_kernel, layer=layer, n_prompt_blocks=npb,
                               split_in=split_in, final=final, sub=sub)
    return pl.pallas_call(
        kernel,
        grid=(npb + nsb,),
        in_specs=in_specs,
        out_specs=out_specs,
        out_shape=out_shape,
        scratch_shapes=_weight_scratch(k, d),
        compiler_params=_params(1),
        name="proj_out_k%d" % k,
    )(*args, *norm_args)


def _ffn_up_kernel(h_ref, wg_ref, wu_ref, a_ref, wbf_ref, *, sub):
    @pl.when(pl.program_id(1) == 0)
    def _():
        wbf_ref[0] = wg_ref[...].astype(BF16)
        wbf_ref[1] = wu_ref[...].astype(BF16)

    for s in range(a_ref.shape[0] // sub):
        rows = pl.ds(s * sub, sub)
        h = h_ref[rows, :]
        g = _dot(h, wbf_ref[0])
        u = _dot(h, wbf_ref[1])
        a_ref[rows, :] = (g * jax.nn.sigmoid(g) * u).astype(a_ref.dtype)


def _ffn_up(h, w_gate, w_up, *, layer, tm, tn, sub):
    m_all, d = h.shape
    f = w_gate.shape[2]
    return pl.pallas_call(
        functools.partial(_ffn_up_kernel, sub=sub),
        grid=(f // tn, m_all // tm),
        in_specs=[
            pl.BlockSpec((tm, d), lambda n, m: (m, 0)),
            pl.BlockSpec((None, d, tn), lambda n, m: (layer, 0, n)),
            pl.BlockSpec((None, d, tn), lambda n, m: (layer, 0, n)),
        ],
        out_specs=pl.BlockSpec((tm, tn), lambda n, m: (m, n)),
        out_shape=jax.ShapeDtypeStruct((m_all, f), BF16),
        scratch_shapes=[pltpu.VMEM((2, d, tn), BF16)],
        compiler_params=_params(2),
        name="ffn_up",
    )(h, w_gate, w_up)


def _sgu_in_kernel(h_ref, w_ref, b_ref, zz_ref, wbf_ref, *, sub):
    @pl.when(pl.program_id(1) == 0)
    def _():
        wbf_ref[...] = w_ref[...].astype(BF16)

    for s in range(zz_ref.shape[0] // sub):
        rows = pl.ds(s * sub, sub)
        z = _dot(h_ref[rows, :], wbf_ref[...]) + b_ref[...]
        gelu = 0.5 * z * (1.0 + lax.erf(z * math.sqrt(0.5)))
        zz_ref[rows, :] = gelu.astype(zz_ref.dtype)


def _sgu_in(h, w, b, *, layer, tm, tn, sub):
    m_all, d = h.shape
    n_out = w.shape[2]
    return pl.pallas_call(
        functools.partial(_sgu_in_kernel, sub=sub),
        grid=(n_out // tn, m_all // tm),
        in_specs=[
            pl.BlockSpec((tm, d), lambda n, m: (m, 0)),
            pl.BlockSpec((None, d, tn), lambda n, m: (layer, 0, n)),
            pl.BlockSpec((None, 1, tn), lambda n, m: (layer, 0, n)),
        ],
        out_specs=pl.BlockSpec((tm, tn), lambda n, m: (m, n)),
        out_shape=jax.ShapeDtypeStruct((m_all, n_out), BF16),
        scratch_shapes=[pltpu.VMEM((d, tn), BF16)],
        compiler_params=_params(2),
        name="sgu_in",
    )(h, w, b)


def _sgu_out_kernel(u_ref, v_ref, lng_ref, lnb_ref, wmix_ref, bmix_ref, w_hbm,
                    x_ref, gpost_ref, gnext_ref,
                    xo_ref, ho_ref, vn_ref,
                    mixw_ref, wbf_ref, stage_ref, sem,
                    *, layer, n_prompt_blocks, dec_seq):
    m = pl.program_id(0)
    tm, d = x_ref.shape
    n_groups = d // SGU_GROUP_DIM
    is_sample = m >= n_prompt_blocks
    kind = is_sample.astype(jnp.int32)

    @pl.when(m == 0)
    def _():
        _load_weight_bf16(w_hbm, wbf_ref, stage_ref, sem, layer=layer)

    @pl.when((m == 0) | (m == n_prompt_blocks))
    def _():
        i = lax.broadcasted_iota(jnp.int32, (SGU_CHUNK, SGU_CHUNK), 0)
        j = lax.broadcasted_iota(jnp.int32, (SGU_CHUNK, SGU_CHUNK), 1)
        same_seq = (i // dec_seq) == (j // dec_seq)
        mask = (i >= j) & (same_seq | jnp.logical_not(is_sample))
        for g in range(n_groups):
            mixw_ref[g] = jnp.where(mask, wmix_ref[kind, g], 0.0).astype(BF16)

    for c in range(tm // SGU_CHUNK):
        rows = pl.ds(c * SGU_CHUNK, SGU_CHUNK)
        v = v_ref[rows, :].astype(F32)
        mu = jnp.mean(v, axis=-1, keepdims=True)
        vc = v - mu
        vn = vc * lax.rsqrt(jnp.mean(vc * vc, axis=-1, keepdims=True) + EPS)
        vn = vn * lng_ref[...] + lnb_ref[...]
        vn_ref[rows, :] = vn
        vnb = vn.astype(BF16)

        gated = []
        for g in range(n_groups):
            cols = slice(g * SGU_GROUP_DIM, (g + 1) * SGU_GROUP_DIM)
            mixed = _dot(mixw_ref[g], vnb[:, cols]) + bmix_ref[kind, :, cols]
            gated.append((u_ref[rows, cols].astype(F32) * mixed).astype(BF16))
        y = _dot(jnp.concatenate(gated, axis=1), wbf_ref[...])

        xn = x_ref[rows, :] + _rms(y, gpost_ref[...])
        xo_ref[rows, :] = xn
        ho_ref[rows, :] = _rms(xn, gnext_ref[...]).astype(ho_ref.dtype)


def _sgu_out(zz, ln_g, ln_b, wmix, bmix, w_out, x, gpost, gnext,
             *, layer, mp, dec_seq, tm):
    m_all, d = x.shape
    ms = m_all - mp
    npb, nsb = mp // tm, ms // tm
    n_groups = d // SGU_GROUP_DIM
    row_block = lambda m: (m, 0)
    const2 = lambda m: (0, 0)
    kernel = functools.partial(_sgu_out_kernel, layer=layer,
                               n_prompt_blocks=npb, dec_seq=dec_seq)
    return pl.pallas_call(
        kernel,
        grid=(npb + nsb,),
        in_specs=[
            pl.BlockSpec((tm, d), lambda m: (m, 0)),
            pl.BlockSpec((tm, d), lambda m: (m, 1)),
            _resident((1, d), const2),
            _resident((1, d), const2),
            _resident((2, n_groups, SGU_CHUNK, SGU_CHUNK),
                      lambda m: (0, 0, 0, 0)),
            _resident((2, SGU_CHUNK, d), lambda m: (0, 0, 0)),
            pl.BlockSpec(memory_space=pl.ANY),
            pl.BlockSpec((tm, d), row_block),
            _resident((1, d), const2),
            _resident((1, d), const2),
        ],
        out_specs=[
            pl.BlockSpec((tm, d), row_block),
            pl.BlockSpec((tm, d), row_block),
            pl.BlockSpec((tm, d), lambda m: (jnp.maximum(m - npb, 0), 0)),
        ],
        out_shape=[
            jax.ShapeDtypeStruct((m_all, d), F32),
            jax.ShapeDtypeStruct((m_all, d), BF16),
            jax.ShapeDtypeStruct((ms, d), F32),
        ],
        scratch_shapes=[pltpu.VMEM((n_groups, SGU_CHUNK, SGU_CHUNK), BF16)]
        + _weight_scratch(d, d),
        compiler_params=_params(1),
        name="sgu_out",
    )(zz, zz, ln_g, ln_b, wmix, bmix, w_out, x, gpost, gnext)


TM_CONV = 1024
TM_SAMPLE = 512
SUB_CONV = 256
TM_FFN = 2176
SUB_FFN = 272
TM_TAIL = 512
SUB_TAIL = 128
TM_DOWN = 256
SUB_DOWN = 128
W_STAGE_ROWS = 256
TN_CONV = 512
TN_FFN = 512
TN_SGU = 1024


def kernel(x_prompt, x_sample, cache_conv, norm_mix_pre, norm_mix_post, norm_ffn_pre, norm_ffn_post, a_w_in, a_conv_w, a_w_out, b_w_in, b_b_in, b_ln_g, b_ln_b, b_w_s, b_b_s, b_w_out, ffn_w_gate, ffn_w_up, ffn_w_down):
    batch, seq, d = x_prompt.shape
    dec_batch, dec_seq, _ = x_sample.shape
    depth = norm_mix_pre.shape[0]
    mp, ms = batch * seq, dec_batch * dec_seq
    assert depth == 2 and seq % TM_CONV == 0 and ms % TM_SAMPLE == 0
    assert SGU_CHUNK % dec_seq == 0 and SUB_CONV % dec_seq == 0 and dec_seq >= CONV_W - 1
    assert mp % TM_TAIL == 0 and ms % TM_TAIL == 0
    assert seq % SGU_CHUNK == 0 and TM_DOWN % SGU_CHUNK == 0 and ms % TM_DOWN == 0
    assert (mp + ms) % TM_FFN == 0

    xp = x_prompt.reshape(mp, d)
    xs = x_sample.reshape(ms, d)
    row = lambda p, i: p[i].reshape(1, -1)

    conv_args = dict(layer=0, seq=seq, dec_seq=dec_seq, tn=TN_CONV, sub=SUB_CONV)
    p_p, tails = _conv_in(xp, row(norm_mix_pre, 0), a_w_in, a_conv_w[0], None,
                          tm=TM_CONV, **conv_args)
    hist = jnp.pad(cache_conv[0], ((0, 0), (0, dec_seq - (CONV_W - 1)), (0, 0)))
    p_s, cz_s = _conv_in(xs, row(norm_mix_pre, 0), a_w_in, a_conv_w[0],
                         hist.reshape(ms, d), tm=TM_SAMPLE, **conv_args)
    state_conv_prompt = tails[None, :, SUBLANES - (CONV_W - 1):, :]
    state_conv_sample = cz_s.reshape(dec_batch, dec_seq, d)[None, :, dec_seq - (CONV_W - 1):, :]
    x, h = _proj_out((p_p, p_s), a_w_out, (xp, xs), row(norm_mix_post, 0),
                     row(norm_ffn_pre, 0), layer=0, mp=mp, tm=TM_TAIL, sub=SUB_TAIL)

    a = _ffn_up(h, ffn_w_gate, ffn_w_up, layer=0, tm=TM_FFN, tn=TN_FFN, sub=SUB_FFN)
    x, h = _proj_out(a, ffn_w_down, x, row(norm_ffn_post, 0),
                     row(norm_mix_pre, 1), layer=0, mp=mp, tm=TM_DOWN, sub=SUB_DOWN)

    zz = _sgu_in(h, b_w_in, b_b_in[:, None, :], layer=0, tm=TM_FFN, tn=TN_SGU,
                 sub=SUB_FFN)
    reps = SGU_CHUNK // dec_seq
    w_s = b_w_s[0]
    wmix = jnp.stack([w_s, jnp.tile(w_s[:, :dec_seq, :dec_seq], (1, reps, reps))])
    b_s = b_b_s[0]
    b_rows = jnp.stack([b_s, jnp.tile(b_s[:, :dec_seq], (1, reps))])
    bmix = jnp.repeat(jnp.swapaxes(b_rows, 1, 2), SGU_GROUP_DIM, axis=2)
    x, h, vn_s = _sgu_out(zz, b_ln_g[0].reshape(1, -1), b_ln_b[0].reshape(1, -1),
                          wmix, bmix, b_w_out, x,
                          row(norm_mix_post, 1), row(norm_ffn_pre, 1),
                          layer=0, mp=mp, dec_seq=dec_seq, tm=TM_DOWN)
    state_sgu_v_sample = vn_s.reshape(1, dec_batch, dec_seq, d)

    a = _ffn_up(h, ffn_w_gate, ffn_w_up, layer=1, tm=TM_FFN, tn=TN_FFN, sub=SUB_FFN)
    yp, ys = _proj_out(a, ffn_w_down, x, row(norm_ffn_post, 1),
                       None, layer=1, mp=mp, tm=TM_DOWN, sub=SUB_DOWN)

    return (yp.reshape(batch, seq, d), ys.reshape(dec_batch, dec_seq, d),
            state_conv_prompt, state_conv_sample, state_sgu_v_sample)
```

```python
import functools
import math

import jax
import jax.numpy as jnp
from jax import lax
from jax.experimental import pallas as pl
from jax.experimental.pallas import tpu as pltpu

EPS = 1e-6
CONV_W = 3
SGU_CHUNK = 128
SGU_GROUP_DIM = 128
SUBLANES = 8

MIB = 1024 * 1024
VMEM_LIMIT_BYTES = 56 * MIB

F32 = jnp.float32
BF16 = jnp.bfloat16


def _params(n_axes):
    return pltpu.CompilerParams(
        dimension_semantics=("arbitrary",) * n_axes,
        vmem_limit_bytes=VMEM_LIMIT_BYTES)


def _rms(x, g):
    y = x * lax.rsqrt(jnp.mean(x * x, axis=-1, keepdims=True) + EPS)
    return y * g


def _dot(a, b):
    return jnp.dot(a, b, preferred_element_type=F32)


def _resident(block_shape, index_map):
    return pl.BlockSpec(block_shape, index_map, pipeline_mode=pl.Buffered(1))


def _conv_in_kernel(*refs, blocks_per_seq, dec_seq, sub, sample):
    if sample:
        (x_ref, g_ref, wb_ref, wc_ref, wz_ref, cw_ref, hist_ref,
         p_ref, state_ref, wbf_ref) = refs
    else:
        (x_ref, g_ref, wb_ref, wc_ref, wz_ref, cw_ref,
         p_ref, state_ref, wbf_ref, carry_ref) = refs
    m = pl.program_id(1)
    tm, tn = p_ref.shape

    @pl.when(m == 0)
    def _():
        wbf_ref[0] = wb_ref[...].astype(BF16)
        wbf_ref[1] = wc_ref[...].astype(BF16)
        wbf_ref[2] = wz_ref[...].astype(BF16)

    if not sample:
        @pl.when(m % blocks_per_seq == 0)
        def _():
            carry_ref[...] = jnp.zeros_like(carry_ref)

        prev = carry_ref[...]

    g = g_ref[...]
    cw = cw_ref[...]
    row = lax.broadcasted_iota(jnp.int32, (sub, tn), 0)
    for s in range(tm // sub):
        rows = pl.ds(s * sub, sub)
        h = _rms(x_ref[rows, :], g).astype(BF16)
        gate_b = _dot(h, wbf_ref[0])
        cz = _dot(h, wbf_ref[1]) * _dot(h, wbf_ref[2])

        if sample:
            hist = hist_ref[rows, :]
            pos = row % dec_seq
            cz1 = jnp.where(pos == 0, pltpu.roll(hist, sub - 1, 0), pltpu.roll(cz, 1, 0))
            cz2 = jnp.where(pos < CONV_W - 1, hist, pltpu.roll(cz, 2, 0))
            state_ref[rows, :] = cz
        else:
            prev1 = prev[SUBLANES - 1:SUBLANES, :]
            prev2 = prev[SUBLANES - 2:SUBLANES - 1, :]
            cz1 = jnp.where(row == 0, prev1, pltpu.roll(cz, 1, 0))
            cz2 = jnp.where(row == 0, prev2,
                            jnp.where(row == 1, prev1, pltpu.roll(cz, 2, 0)))
            prev = cz[sub - SUBLANES:, :]

        conv = cw[0:1, :] * cz2 + cw[1:2, :] * cz1 + cw[2:3, :] * cz
        p_ref[rows, :] = (gate_b * conv).astype(p_ref.dtype)

    if not sample:
        carry_ref[...] = prev
        state_ref[0] = prev


def _conv_in(x, g, w_in, conv_w, hist, *, layer, seq, dec_seq, tm, tn, sub):
    sample = hist is not None
    rows, d = x.shape
    nt = d // tn
    bps = seq // tm
    kernel = functools.partial(_conv_in_kernel, blocks_per_seq=bps,
                               dec_seq=dec_seq, sub=sub, sample=sample)
    row_tile = pl.BlockSpec((tm, tn), lambda n, m: (m, n))
    in_specs = [
        pl.BlockSpec((tm, d), lambda n, m: (m, 0)),
        pl.BlockSpec((1, d), lambda n, m: (0, 0)),
        pl.BlockSpec((None, d, tn), lambda n, m: (layer, 0, n)),
        pl.BlockSpec((None, d, tn), lambda n, m: (layer, 0, nt + n)),
        pl.BlockSpec((None, d, tn), lambda n, m: (layer, 0, 2 * nt + n)),
        pl.BlockSpec((CONV_W, tn), lambda n, m: (0, n)),
    ]
    scratch = [pltpu.VMEM((3, d, tn), BF16)]
    if sample:
        in_specs.append(row_tile)
        args = (x, g, w_in, w_in, w_in, conv_w, hist)
        state_spec = row_tile
        state_shape = jax.ShapeDtypeStruct((rows, d), F32)
    else:
        args = (x, g, w_in, w_in, w_in, conv_w)
        state_spec = pl.BlockSpec((1, SUBLANES, tn), lambda n, m: (m // bps, 0, n))
        state_shape = jax.ShapeDtypeStruct((rows // seq, SUBLANES, d), F32)
        scratch.append(pltpu.VMEM((SUBLANES, tn), F32))
    return pl.pallas_call(
        kernel,
        grid=(nt, rows // tm),
        in_specs=in_specs,
        out_specs=[row_tile, state_spec],
        out_shape=[jax.ShapeDtypeStruct((rows, d), BF16), state_shape],
        scratch_shapes=scratch,
        compiler_params=_params(2),
        name="conv_in_sample" if sample else "conv_in_prompt",
    )(*args)


def _load_weight_bf16(w_hbm, wbf_ref, stage_ref, sem, *, layer):
    chunk = stage_ref.shape[1]
    n_chunks = wbf_ref.shape[0] // chunk

    def copy(c):
        return pltpu.make_async_copy(
            w_hbm.at[layer, pl.ds(c * chunk, chunk), :],
            stage_ref.at[c % 2], sem.at[c % 2])

    copy(0).start()
    for c in range(n_chunks):
        if c + 1 < n_chunks:
            copy(c + 1).start()
        copy(c).wait()
        wbf_ref[pl.ds(c * chunk, chunk), :] = stage_ref[c % 2].astype(BF16)


def _weight_scratch(k, d, tm):
    stage_rows = W_STAGE_ROWS // 2 if tm > TM_DOWN else W_STAGE_ROWS
    return [pltpu.VMEM((k, d), BF16),
            pltpu.VMEM((2, stage_rows, d), F32),
            pltpu.SemaphoreType.DMA((2,))]


def _proj_out_kernel(*refs, layer, n_prompt_blocks, split_in, final, sub):
    m = pl.program_id(0)
    refs = list(refs)
    wbf_ref, stage_ref, sem = refs[-3:]
    refs = refs[:-3]
    if split_in:
        ap_ref, as_ref, w_hbm, xp_ref, xs_ref = refs[:5]
        refs = refs[5:]
    else:
        a_ref, w_hbm, x_ref = refs[:3]
        refs = refs[3:]
    gpost_ref = refs[0]
    refs = refs[1:]
    if not final:
        gnext_ref = refs[0]
        refs = refs[1:]
    o0_ref, o1_ref = refs

    @pl.when(m == 0)
    def _():
        _load_weight_bf16(w_hbm, wbf_ref, stage_ref, sem, layer=layer)

    is_prompt = m < n_prompt_blocks

    def run(a_ref, x_ref, xo_ref, ho_ref):
        for s in range(a_ref.shape[0] // sub):
            rows = pl.ds(s * sub, sub)
            y = _dot(a_ref[rows, :], wbf_ref[...])
            xn = x_ref[rows, :] + _rms(y, gpost_ref[...])
            xo_ref[rows, :] = xn
            if ho_ref is not None:
                ho_ref[rows, :] = _rms(xn, gnext_ref[...]).astype(ho_ref.dtype)

    if split_in:
        pl.when(is_prompt)(lambda: run(ap_ref, xp_ref, o0_ref, o1_ref))
        pl.when(jnp.logical_not(is_prompt))(lambda: run(as_ref, xs_ref, o0_ref, o1_ref))
    elif final:
        pl.when(is_prompt)(lambda: run(a_ref, x_ref, o0_ref, None))
        pl.when(jnp.logical_not(is_prompt))(lambda: run(a_ref, x_ref, o1_ref, None))
    else:
        run(a_ref, x_ref, o0_ref, o1_ref)


def _proj_out(a, w, res, gpost, gnext, *, layer, mp, tm, sub):
    final = gnext is None
    split_in = isinstance(a, tuple)
    assert split_in == isinstance(res, tuple)
    k, d = w.shape[1:]
    m_all = sum(t.shape[0] for t in a) if split_in else a.shape[0]
    ms = m_all - mp
    npb, nsb = mp // tm, ms // tm
    prompt_block = lambda m: (jnp.minimum(m, npb - 1), 0)
    sample_block = lambda m: (jnp.maximum(m - npb, 0), 0)
    row_block = lambda m: (m, 0)
    w_spec = pl.BlockSpec(memory_space=pl.ANY)
    if split_in:
        in_specs = [pl.BlockSpec((tm, k), prompt_block), pl.BlockSpec((tm, k), sample_block),
                    w_spec,
                    pl.BlockSpec((tm, d), prompt_block), pl.BlockSpec((tm, d), sample_block)]
        args = (*a, w, *res)
    else:
        in_specs = [pl.BlockSpec((tm, k), row_block), w_spec,
                    pl.BlockSpec((tm, d), row_block)]
        args = (a, w, res)
    norm_args = (gpost,) if final else (gpost, gnext)
    in_specs += [_resident((1, d), lambda m: (0, 0)) for _ in norm_args]
    if final:
        out_specs = [pl.BlockSpec((tm, d), prompt_block),
                     pl.BlockSpec((tm, d), sample_block)]
        out_shape = [jax.ShapeDtypeStruct((mp, d), F32),
                     jax.ShapeDtypeStruct((ms, d), F32)]
    else:
        out_specs = [pl.BlockSpec((tm, d), row_block),
                     pl.BlockSpec((tm, d), row_block)]
        out_shape = [jax.ShapeDtypeStruct((m_all, d), F32),
                     jax.ShapeDtypeStruct((m_all, d), BF16)]
    kernel = functools.partial(_proj_out_kernel, layer=layer, n_prompt_blocks=npb,
                               split_in=split_in, final=final, sub=sub)
    return pl.pallas_call(
        kernel,
        grid=(npb + nsb,),
        in_specs=in_specs,
        out_specs=out_specs,
        out_shape=out_shape,
        scratch_shapes=_weight_scratch(k, d, tm),
        compiler_params=_params(1),
        name="proj_out_k%d" % k,
    )(*args, *norm_args)


def _ffn_up_kernel(h_ref, wg_ref, wu_ref, a_ref, wbf_ref, *, sub):
    @pl.when(pl.program_id(1) == 0)
    def _():
        wbf_ref[0] = wg_ref[...].astype(BF16)
        wbf_ref[1] = wu_ref[...].astype(BF16)

    for s in range(a_ref.shape[0] // sub):
        rows = pl.ds(s * sub, sub)
        h = h_ref[rows, :]
        g = _dot(h, wbf_ref[0])
        u = _dot(h, wbf_ref[1])
        a_ref[rows, :] = (g * jax.nn.sigmoid(g) * u).astype(a_ref.dtype)


def _ffn_up(h, w_gate, w_up, *, layer, tm, tn, sub):
    m_all, d = h.shape
    f = w_gate.shape[2]
    return pl.pallas_call(
        functools.partial(_ffn_up_kernel, sub=sub),
        grid=(f // tn, m_all // tm),
        in_specs=[
            pl.BlockSpec((tm, d), lambda n, m: (m, 0)),
            pl.BlockSpec((None, d, tn), lambda n, m: (layer, 0, n)),
            pl.BlockSpec((None, d, tn), lambda n, m: (layer, 0, n)),
        ],
        out_specs=pl.BlockSpec((tm, tn), lambda n, m: (m, n)),
        out_shape=jax.ShapeDtypeStruct((m_all, f), BF16),
        scratch_shapes=[pltpu.VMEM((2, d, tn), BF16)],
        compiler_params=_params(2),
        name="ffn_up",
    )(h, w_gate, w_up)


def _sgu_in_kernel(h_ref, w_ref, b_ref, zz_ref, wbf_ref, *, sub):
    @pl.when(pl.program_id(1) == 0)
    def _():
        wbf_ref[...] = w_ref[...].astype(BF16)

    for s in range(zz_ref.shape[0] // sub):
        rows = pl.ds(s * sub, sub)
        z = _dot(h_ref[rows, :], wbf_ref[...]) + b_ref[...]
        gelu = 0.5 * z * (1.0 + lax.erf(z * math.sqrt(0.5)))
        zz_ref[rows, :] = gelu.astype(zz_ref.dtype)


def _sgu_in(h, w, b, *, layer, tm, tn, sub):
    m_all, d = h.shape
    n_out = w.shape[2]
    return pl.pallas_call(
        functools.partial(_sgu_in_kernel, sub=sub),
        grid=(n_out // tn, m_all // tm),
        in_specs=[
            pl.BlockSpec((tm, d), lambda n, m: (m, 0)),
            pl.BlockSpec((None, d, tn), lambda n, m: (layer, 0, n)),
            pl.BlockSpec((None, 1, tn), lambda n, m: (layer, 0, n)),
        ],
        out_specs=pl.BlockSpec((tm, tn), lambda n, m: (m, n)),
        out_shape=jax.ShapeDtypeStruct((m_all, n_out), BF16),
        scratch_shapes=[pltpu.VMEM((d, tn), BF16)],
        compiler_params=_params(2),
        name="sgu_in",
    )(h, w, b)


def _sgu_out_kernel(u_ref, v_ref, lng_ref, lnb_ref, wmix_ref, bcol_ref, w_hbm,
                    x_ref, gpost_ref, gnext_ref,
                    xo_ref, ho_ref, vn_ref,
                    mixw_ref, mixb_ref, wbf_ref, stage_ref, sem,
                    *, layer, n_prompt_blocks, dec_seq):
    m = pl.program_id(0)
    tm, d = x_ref.shape
    n_groups = d // SGU_GROUP_DIM
    is_sample = m >= n_prompt_blocks
    kind = is_sample.astype(jnp.int32)

    @pl.when(m == 0)
    def _():
        _load_weight_bf16(w_hbm, wbf_ref, stage_ref, sem, layer=layer)

    @pl.when((m == 0) | (m == n_prompt_blocks))
    def _():
        i = lax.broadcasted_iota(jnp.int32, (SGU_CHUNK, SGU_CHUNK), 0)
        j = lax.broadcasted_iota(jnp.int32, (SGU_CHUNK, SGU_CHUNK), 1)
        same_seq = (i // dec_seq) == (j // dec_seq)
        mask = (i >= j) & (same_seq | jnp.logical_not(is_sample))
        bcol = bcol_ref[kind]
        for g in range(n_groups):
            cols = slice(g * SGU_GROUP_DIM, (g + 1) * SGU_GROUP_DIM)
            mixw_ref[g] = jnp.where(mask, wmix_ref[kind, g], 0.0).astype(BF16)
            mixb_ref[:, cols] = jnp.broadcast_to(bcol[:, g:g + 1],
                                                 (SGU_CHUNK, SGU_GROUP_DIM))

    for c in range(tm // SGU_CHUNK):
        rows = pl.ds(c * SGU_CHUNK, SGU_CHUNK)
        v = v_ref[rows, :].astype(F32)
        mu = jnp.mean(v, axis=-1, keepdims=True)
        vc = v - mu
        vn = vc * lax.rsqrt(jnp.mean(vc * vc, axis=-1, keepdims=True) + EPS)
        vn = vn * lng_ref[...] + lnb_ref[...]
        vn_ref[rows, :] = vn
        vnb = vn.astype(BF16)

        gated = []
        for g in range(n_groups):
            cols = slice(g * SGU_GROUP_DIM, (g + 1) * SGU_GROUP_DIM)
            mixed = _dot(mixw_ref[g], vnb[:, cols]) + mixb_ref[:, cols]
            gated.append((u_ref[rows, cols].astype(F32) * mixed).astype(BF16))
        y = _dot(jnp.concatenate(gated, axis=1), wbf_ref[...])

        xn = x_ref[rows, :] + _rms(y, gpost_ref[...])
        xo_ref[rows, :] = xn
        ho_ref[rows, :] = _rms(xn, gnext_ref[...]).astype(ho_ref.dtype)


def _sgu_out(zz, ln_g, ln_b, wmix, bcol, w_out, x, gpost, gnext,
             *, layer, mp, dec_seq, tm):
    m_all, d = x.shape
    ms = m_all - mp
    npb, nsb = mp // tm, ms // tm
    n_groups = d // SGU_GROUP_DIM
    row_block = lambda m: (m, 0)
    const2 = lambda m: (0, 0)
    kernel = functools.partial(_sgu_out_kernel, layer=layer,
                               n_prompt_blocks=npb, dec_seq=dec_seq)
    return pl.pallas_call(
        kernel,
        grid=(npb + nsb,),
        in_specs=[
            pl.BlockSpec((tm, d), lambda m: (m, 0)),
            pl.BlockSpec((tm, d), lambda m: (m, 1)),
            _resident((1, d), const2),
            _resident((1, d), const2),
            _resident((2, n_groups, SGU_CHUNK, SGU_CHUNK),
                      lambda m: (0, 0, 0, 0)),
            _resident((2, SGU_CHUNK, n_groups), lambda m: (0, 0, 0)),
            pl.BlockSpec(memory_space=pl.ANY),
            pl.BlockSpec((tm, d), row_block),
            _resident((1, d), const2),
            _resident((1, d), const2),
        ],
        out_specs=[
            pl.BlockSpec((tm, d), row_block),
            pl.BlockSpec((tm, d), row_block),
            pl.BlockSpec((tm, d), lambda m: (jnp.maximum(m - npb, 0), 0)),
        ],
        out_shape=[
            jax.ShapeDtypeStruct((m_all, d), F32),
            jax.ShapeDtypeStruct((m_all, d), BF16),
            jax.ShapeDtypeStruct((ms, d), F32),
        ],
        scratch_shapes=[pltpu.VMEM((n_groups, SGU_CHUNK, SGU_CHUNK), BF16),
                        pltpu.VMEM((SGU_CHUNK, d), F32)]
        + _weight_scratch(d, d, tm),
        compiler_params=_params(1),
        name="sgu_out",
    )(zz, zz, ln_g, ln_b, wmix, bcol, w_out, x, gpost, gnext)


TM_CONV = 1024
TM_SAMPLE = 512
SUB_CONV = 256
TM_FFN = 2176
SUB_FFN = 544
TM_TAIL = 512
SUB_TAIL = 128
TM_DOWN = 256
SUB_DOWN = 128
W_STAGE_ROWS = 512
TN_CONV = 512
TN_FFN = 512
TN_SGU = 1024


def kernel(x_prompt, x_sample, cache_conv, norm_mix_pre, norm_mix_post, norm_ffn_pre, norm_ffn_post, a_w_in, a_conv_w, a_w_out, b_w_in, b_b_in, b_ln_g, b_ln_b, b_w_s, b_b_s, b_w_out, ffn_w_gate, ffn_w_up, ffn_w_down):
    batch, seq, d = x_prompt.shape
    dec_batch, dec_seq, _ = x_sample.shape
    depth = norm_mix_pre.shape[0]
    mp, ms = batch * seq, dec_batch * dec_seq
    assert depth == 2 and seq % TM_CONV == 0 and ms % TM_SAMPLE == 0
    assert SGU_CHUNK % dec_seq == 0 and SUB_CONV % dec_seq == 0 and dec_seq >= CONV_W - 1
    assert mp % TM_TAIL == 0 and ms % TM_TAIL == 0
    assert seq % SGU_CHUNK == 0 and TM_DOWN % SGU_CHUNK == 0 and ms % TM_DOWN == 0
    assert (mp + ms) % TM_FFN == 0

    xp = x_prompt.reshape(mp, d)
    xs = x_sample.reshape(ms, d)
    row = lambda p, i: p[i].reshape(1, -1)

    conv_args = dict(layer=0, seq=seq, dec_seq=dec_seq, tn=TN_CONV, sub=SUB_CONV)
    p_p, tails = _conv_in(xp, row(norm_mix_pre, 0), a_w_in, a_conv_w[0], None,
                          tm=TM_CONV, **conv_args)
    hist = jnp.pad(cache_conv[0], ((0, 0), (0, dec_seq - (CONV_W - 1)), (0, 0)))
    p_s, cz_s = _conv_in(xs, row(norm_mix_pre, 0), a_w_in, a_conv_w[0],
                         hist.reshape(ms, d), tm=TM_SAMPLE, **conv_args)
    state_conv_prompt = tails[None, :, SUBLANES - (CONV_W - 1):, :]
    state_conv_sample = cz_s.reshape(dec_batch, dec_seq, d)[None, :, dec_seq - (CONV_W - 1):, :]
    x, h = _proj_out((p_p, p_s), a_w_out, (xp, xs), row(norm_mix_post, 0),
                     row(norm_ffn_pre, 0), layer=0, mp=mp, tm=TM_TAIL, sub=SUB_TAIL)

    a = _ffn_up(h, ffn_w_gate, ffn_w_up, layer=0, tm=TM_FFN, tn=TN_FFN, sub=SUB_FFN)
    x, h = _proj_out(a, ffn_w_down, x, row(norm_ffn_post, 0),
                     row(norm_mix_pre, 1), layer=0, mp=mp, tm=TM_DOWN, sub=SUB_DOWN)

    zz = _sgu_in(h, b_w_in, b_b_in[:, None, :], layer=0, tm=TM_FFN, tn=TN_SGU,
                 sub=SUB_FFN)
    reps = SGU_CHUNK // dec_seq
    w_s = b_w_s[0]
    wmix = jnp.stack([w_s, jnp.tile(w_s[:, :dec_seq, :dec_seq], (1, reps, reps))])
    b_s = b_b_s[0]
    b_rows = jnp.stack([b_s, jnp.tile(b_s[:, :dec_seq], (1, reps))])
    bcol = jnp.swapaxes(b_rows, 1, 2)
    x, h, vn_s = _sgu_out(zz, b_ln_g[0].reshape(1, -1), b_ln_b[0].reshape(1, -1),
                          wmix, bcol, b_w_out, x,
                          row(norm_mix_post, 1), row(norm_ffn_pre, 1),
                          layer=0, mp=mp, dec_seq=dec_seq, tm=TM_DOWN)
    state_sgu_v_sample = vn_s.reshape(1, dec_batch, dec_seq, d)

    a = _ffn_up(h, ffn_w_gate, ffn_w_up, layer=1, tm=TM_FFN, tn=TN_FFN, sub=SUB_FFN)
    yp, ys = _proj_out(a, ffn_w_down, x, row(norm_ffn_post, 1),
                       None, layer=1, mp=mp, tm=TM_DOWN, sub=SUB_DOWN)

    return (yp.reshape(batch, seq, d), ys.reshape(dec_batch, dec_seq, d),
            state_conv_prompt, state_conv_sample, state_sgu_v_sample)
```

```python
import functools
import math

import jax
import jax.numpy as jnp
from jax import lax
from jax.experimental import pallas as pl
from jax.experimental.pallas import tpu as pltpu

EPS = 1e-6
CONV_W = 3
SGU_CHUNK = 128
SGU_GROUP_DIM = 128
SUBLANES = 8

MIB = 1024 * 1024
VMEM_LIMIT_BYTES = 56 * MIB

F32 = jnp.float32
BF16 = jnp.bfloat16


def _params(n_axes):
    return pltpu.CompilerParams(
        dimension_semantics=("arbitrary",) * n_axes,
        vmem_limit_bytes=VMEM_LIMIT_BYTES)


def _rms(x, g):
    y = x * lax.rsqrt(jnp.mean(x * x, axis=-1, keepdims=True) + EPS)
    return y * g


def _dot(a, b):
    return jnp.dot(a, b, preferred_element_type=F32)


def _resident(block_shape, index_map):
    return pl.BlockSpec(block_shape, index_map, pipeline_mode=pl.Buffered(1))


def _conv_in_kernel(*refs, blocks_per_seq, dec_seq, sub, sample):
    if sample:
        (x_ref, g_ref, wb_ref, wc_ref, wz_ref, cw_ref, hist_ref,
         p_ref, state_ref, wbf_ref) = refs
    else:
        (x_ref, g_ref, wb_ref, wc_ref, wz_ref, cw_ref,
         p_ref, state_ref, wbf_ref, carry_ref) = refs
    m = pl.program_id(1)
    tm, tn = p_ref.shape

    @pl.when(m == 0)
    def _():
        wbf_ref[0] = wb_ref[...].astype(BF16)
        wbf_ref[1] = wc_ref[...].astype(BF16)
        wbf_ref[2] = wz_ref[...].astype(BF16)

    if not sample:
        @pl.when(m % blocks_per_seq == 0)
        def _():
            carry_ref[...] = jnp.zeros_like(carry_ref)

        prev = carry_ref[...]

    g = g_ref[...]
    cw = cw_ref[...]
    row = lax.broadcasted_iota(jnp.int32, (sub, tn), 0)
    for s in range(tm // sub):
        rows = pl.ds(s * sub, sub)
        h = _rms(x_ref[rows, :], g).astype(BF16)
        gate_b = _dot(h, wbf_ref[0])
        cz = _dot(h, wbf_ref[1]) * _dot(h, wbf_ref[2])

        if sample:
            hist = hist_ref[rows, :]
            pos = row % dec_seq
            cz1 = jnp.where(pos == 0, pltpu.roll(hist, sub - 1, 0), pltpu.roll(cz, 1, 0))
            cz2 = jnp.where(pos < CONV_W - 1, hist, pltpu.roll(cz, 2, 0))
            state_ref[rows, :] = cz
        else:
            prev1 = prev[SUBLANES - 1:SUBLANES, :]
            prev2 = prev[SUBLANES - 2:SUBLANES - 1, :]
            cz1 = jnp.where(row == 0, prev1, pltpu.roll(cz, 1, 0))
            cz2 = jnp.where(row == 0, prev2,
                            jnp.where(row == 1, prev1, pltpu.roll(cz, 2, 0)))
            prev = cz[sub - SUBLANES:, :]

        conv = cw[0:1, :] * cz2 + cw[1:2, :] * cz1 + cw[2:3, :] * cz
        p_ref[rows, :] = (gate_b * conv).astype(p_ref.dtype)

    if not sample:
        carry_ref[...] = prev
        state_ref[0] = prev


def _conv_in(x, g, w_in, conv_w, hist, *, layer, seq, dec_seq, tm, tn, sub):
    sample = hist is not None
    rows, d = x.shape
    nt = d // tn
    bps = seq // tm
    kernel = functools.partial(_conv_in_kernel, blocks_per_seq=bps,
                               dec_seq=dec_seq, sub=sub, sample=sample)
    row_tile = pl.BlockSpec((tm, tn), lambda n, m: (m, n))
    in_specs = [
        pl.BlockSpec((tm, d), lambda n, m: (m, 0)),
        pl.BlockSpec((1, d), lambda n, m: (0, 0)),
        pl.BlockSpec((None, d, tn), lambda n, m: (layer, 0, n)),
        pl.BlockSpec((None, d, tn), lambda n, m: (layer, 0, nt + n)),
        pl.BlockSpec((None, d, tn), lambda n, m: (layer, 0, 2 * nt + n)),
        pl.BlockSpec((CONV_W, tn), lambda n, m: (0, n)),
    ]
    scratch = [pltpu.VMEM((3, d, tn), BF16)]
    if sample:
        in_specs.append(row_tile)
        args = (x, g, w_in, w_in, w_in, conv_w, hist)
        state_spec = row_tile
        state_shape = jax.ShapeDtypeStruct((rows, d), F32)
    else:
        args = (x, g, w_in, w_in, w_in, conv_w)
        state_spec = pl.BlockSpec((1, SUBLANES, tn), lambda n, m: (m // bps, 0, n))
        state_shape = jax.ShapeDtypeStruct((rows // seq, SUBLANES, d), F32)
        scratch.append(pltpu.VMEM((SUBLANES, tn), F32))
    return pl.pallas_call(
        kernel,
        grid=(nt, rows // tm),
        in_specs=in_specs,
        out_specs=[row_tile, state_spec],
        out_shape=[jax.ShapeDtypeStruct((rows, d), BF16), state_shape],
        scratch_shapes=scratch,
        compiler_params=_params(2),
        name="conv_in_sample" if sample else "conv_in_prompt",
    )(*args)


def _load_weight_bf16(w_hbm, wbf_ref, stage_ref, sem, *, layer):
    chunk = stage_ref.shape[1]
    n_chunks = wbf_ref.shape[0] // chunk

    def copy(c):
        return pltpu.make_async_copy(
            w_hbm.at[layer, pl.ds(c * chunk, chunk), :],
            stage_ref.at[c % 2], sem.at[c % 2])

    copy(0).start()
    for c in range(n_chunks):
        if c + 1 < n_chunks:
            copy(c + 1).start()
        copy(c).wait()
        wbf_ref[pl.ds(c * chunk, chunk), :] = stage_ref[c % 2].astype(BF16)


def _weight_scratch(k, d, tm):
    stage_rows = W_STAGE_ROWS // 2 if tm > TM_DOWN else W_STAGE_ROWS
    return [pltpu.VMEM((k, d), BF16),
            pltpu.VMEM((2, stage_rows, d), F32),
            pltpu.SemaphoreType.DMA((2,))]


def _proj_out_kernel(*refs, layer, n_prompt_blocks, split_in, final, sub):
    m = pl.program_id(0)
    refs = list(refs)
    wbf_ref, stage_ref, sem = refs[-3:]
    refs = refs[:-3]
    if split_in:
        ap_ref, as_ref, w_hbm, xp_ref, xs_ref = refs[:5]
        refs = refs[5:]
    else:
        a_ref, w_hbm, x_ref = refs[:3]
        refs = refs[3:]
    gpost_ref = refs[0]
    refs = refs[1:]
    if not final:
        gnext_ref = refs[0]
        refs = refs[1:]
    o0_ref, o1_ref = refs

    @pl.when(m == 0)
    def _():
        _load_weight_bf16(w_hbm, wbf_ref, stage_ref, sem, layer=layer)

    is_prompt = m < n_prompt_blocks

    def run(a_ref, x_ref, xo_ref, ho_ref):
        for s in range(a_ref.shape[0] // sub):
            rows = pl.ds(s * sub, sub)
            y = _dot(a_ref[rows, :], wbf_ref[...])
            xn = x_ref[rows, :] + _rms(y, gpost_ref[...])
            xo_ref[rows, :] = xn
            if ho_ref is not None:
                ho_ref[rows, :] = _rms(xn, gnext_ref[...]).astype(ho_ref.dtype)

    if split_in:
        pl.when(is_prompt)(lambda: run(ap_ref, xp_ref, o0_ref, o1_ref))
        pl.when(jnp.logical_not(is_prompt))(lambda: run(as_ref, xs_ref, o0_ref, o1_ref))
    elif final:
        pl.when(is_prompt)(lambda: run(a_ref, x_ref, o0_ref, None))
        pl.when(jnp.logical_not(is_prompt))(lambda: run(a_ref, x_ref, o1_ref, None))
    else:
        run(a_ref, x_ref, o0_ref, o1_ref)


def _proj_out(a, w, res, gpost, gnext, *, layer, mp, tm, sub):
    final = gnext is None
    split_in = isinstance(a, tuple)
    assert split_in == isinstance(res, tuple)
    k, d = w.shape[1:]
    m_all = sum(t.shape[0] for t in a) if split_in else a.shape[0]
    ms = m_all - mp
    npb, nsb = mp // tm, ms // tm
    prompt_block = lambda m: (jnp.minimum(m, npb - 1), 0)
    sample_block = lambda m: (jnp.maximum(m - npb, 0), 0)
    row_block = lambda m: (m, 0)
    w_spec = pl.BlockSpec(memory_space=pl.ANY)
    if split_in:
        in_specs = [pl.BlockSpec((tm, k), prompt_block), pl.BlockSpec((tm, k), sample_block),
                    w_spec,
                    pl.BlockSpec((tm, d), prompt_block), pl.BlockSpec((tm, d), sample_block)]
        args = (*a, w, *res)
    else:
        in_specs = [pl.BlockSpec((tm, k), row_block), w_spec,
                    pl.BlockSpec((tm, d), row_block)]
        args = (a, w, res)
    norm_args = (gpost,) if final else (gpost, gnext)
    in_specs += [_resident((1, d), lambda m: (0, 0)) for _ in norm_args]
    if final:
        out_specs = [pl.BlockSpec((tm, d), prompt_block),
                     pl.BlockSpec((tm, d), sample_block)]
        out_shape = [jax.ShapeDtypeStruct((mp, d), F32),
                     jax.ShapeDtypeStruct((ms, d), F32)]
    else:
        out_specs = [pl.BlockSpec((tm, d), row_block),
                     pl.BlockSpec((tm, d), row_block)]
        out_shape = [jax.ShapeDtypeStruct((m_all, d), F32),
                     jax.ShapeDtypeStruct((m_all, d), BF16)]
    kernel = functools.partial(_proj_out_kernel, layer=layer, n_prompt_blocks=npb,
                               split_in=split_in, final=final, sub=sub)
    return pl.pallas_call(
        kernel,
        grid=(npb + nsb,),
        in_specs=in_specs,
        out_specs=out_specs,
        out_shape=out_shape,
        scratch_shapes=_weight_scratch(k, d, tm),
        compiler_params=_params(1),
        name="proj_out_k%d" % k,
    )(*args, *norm_args)


def _ffn_up_kernel(h_ref, wg_ref, wu_ref, a_ref, wbf_ref, *, sub):
    @pl.when(pl.program_id(1) == 0)
    def _():
        wbf_ref[0] = wg_ref[...].astype(BF16)
        wbf_ref[1] = wu_ref[...].astype(BF16)

    for s in range(a_ref.shape[0] // sub):
        rows = pl.ds(s * sub, sub)
        h = h_ref[rows, :]
        g = _dot(h, wbf_ref[0])
        u = _dot(h, wbf_ref[1])
        a_ref[rows, :] = (g * jax.nn.sigmoid(g) * u).astype(a_ref.dtype)


def _ffn_up(h, w_gate, w_up, *, layer, tm, tn, sub):
    m_all, d = h.shape
    f = w_gate.shape[2]
    return pl.pallas_call(
        functools.partial(_ffn_up_kernel, sub=sub),
        grid=(f // tn, m_all // tm),
        in_specs=[
            pl.BlockSpec((tm, d), lambda n, m: (m, 0)),
            pl.BlockSpec((None, d, tn), lambda n, m: (layer, 0, n)),
            pl.BlockSpec((None, d, tn), lambda n, m: (layer, 0, n)),
        ],
        out_specs=pl.BlockSpec((tm, tn), lambda n, m: (m, n)),
        out_shape=jax.ShapeDtypeStruct((m_all, f), BF16),
        scratch_shapes=[pltpu.VMEM((2, d, tn), BF16)],
        compiler_params=_params(2),
        name="ffn_up",
    )(h, w_gate, w_up)


def _sgu_in_kernel(h_ref, w_ref, b_ref, zz_ref, wbf_ref, *, sub):
    @pl.when(pl.program_id(1) == 0)
    def _():
        wbf_ref[...] = w_ref[...].astype(BF16)

    for s in range(zz_ref.shape[0] // sub):
        rows = pl.ds(s * sub, sub)
        z = _dot(h_ref[rows, :], wbf_ref[...]) + b_ref[...]
        gelu = 0.5 * z * (1.0 + lax.erf(z * math.sqrt(0.5)))
        zz_ref[rows, :] = gelu.astype(zz_ref.dtype)


def _sgu_in(h, w, b, *, layer, tm, tn, sub):
    m_all, d = h.shape
    n_out = w.shape[2]
    return pl.pallas_call(
        functools.partial(_sgu_in_kernel, sub=sub),
        grid=(n_out // tn, m_all // tm),
        in_specs=[
            pl.BlockSpec((tm, d), lambda n, m: (m, 0)),
            pl.BlockSpec((None, d, tn), lambda n, m: (layer, 0, n)),
            pl.BlockSpec((None, 1, tn), lambda n, m: (layer, 0, n)),
        ],
        out_specs=pl.BlockSpec((tm, tn), lambda n, m: (m, n)),
        out_shape=jax.ShapeDtypeStruct((m_all, n_out), BF16),
        scratch_shapes=[pltpu.VMEM((d, tn), BF16)],
        compiler_params=_params(2),
        name="sgu_in",
    )(h, w, b)


def _sgu_out_kernel(u_ref, v_ref, lng_ref, lnb_ref, ws_ref, bcol_ref, w_hbm,
                    x_ref, gpost_ref, gnext_ref,
                    xo_ref, ho_ref, vn_ref,
                    mixw_ref, mixb_ref, wbf_ref, stage_ref, sem,
                    *, layer, n_prompt_blocks, dec_seq):
    m = pl.program_id(0)
    tm, d = x_ref.shape
    n_groups = d // SGU_GROUP_DIM

    @pl.when(m == 0)
    def _():
        _load_weight_bf16(w_hbm, wbf_ref, stage_ref, sem, layer=layer)

    def build_mixing(sample_kind):
        i = lax.broadcasted_iota(jnp.int32, (SGU_CHUNK, SGU_CHUNK), 0)
        j = lax.broadcasted_iota(jnp.int32, (SGU_CHUNK, SGU_CHUNK), 1)
        mask = i >= j
        if sample_kind:
            mask = mask & ((i // dec_seq) == (j // dec_seq))
        reps = SGU_CHUNK // dec_seq
        bcol = bcol_ref[int(sample_kind)]
        for g in range(n_groups):
            cols = slice(g * SGU_GROUP_DIM, (g + 1) * SGU_GROUP_DIM)
            if sample_kind:
                w = ws_ref[g, :dec_seq, :dec_seq]
                w = jnp.concatenate([w] * reps, axis=0)
                w = jnp.concatenate([w] * reps, axis=1)
            else:
                w = ws_ref[g]
            mixw_ref[g] = jnp.where(mask, w, 0.0).astype(BF16)
            mixb_ref[:, cols] = jnp.broadcast_to(bcol[:, g:g + 1],
                                                 (SGU_CHUNK, SGU_GROUP_DIM))

    pl.when(m == 0)(lambda: build_mixing(False))
    pl.when(m == n_prompt_blocks)(lambda: build_mixing(True))

    for c in range(tm // SGU_CHUNK):
        rows = pl.ds(c * SGU_CHUNK, SGU_CHUNK)
        v = v_ref[rows, :].astype(F32)
        mu = jnp.mean(v, axis=-1, keepdims=True)
        vc = v - mu
        vn = vc * lax.rsqrt(jnp.mean(vc * vc, axis=-1, keepdims=True) + EPS)
        vn = vn * lng_ref[...] + lnb_ref[...]
        vn_ref[rows, :] = vn
        vnb = vn.astype(BF16)

        gated = []
        for g in range(n_groups):
            cols = slice(g * SGU_GROUP_DIM, (g + 1) * SGU_GROUP_DIM)
            mixed = _dot(mixw_ref[g], vnb[:, cols]) + mixb_ref[:, cols]
            gated.append((u_ref[rows, cols].astype(F32) * mixed).astype(BF16))
        y = _dot(jnp.concatenate(gated, axis=1), wbf_ref[...])

        xn = x_ref[rows, :] + _rms(y, gpost_ref[...])
        xo_ref[rows, :] = xn
        ho_ref[rows, :] = _rms(xn, gnext_ref[...]).astype(ho_ref.dtype)


def _sgu_out(zz, ln_g, ln_b, w_s, bcol, w_out, x, gpost, gnext,
             *, layer, mp, dec_seq, tm):
    m_all, d = x.shape
    ms = m_all - mp
    npb, nsb = mp // tm, ms // tm
    n_groups = d // SGU_GROUP_DIM
    row_block = lambda m: (m, 0)
    const2 = lambda m: (0, 0)
    kernel = functools.partial(_sgu_out_kernel, layer=layer,
                               n_prompt_blocks=npb, dec_seq=dec_seq)
    return pl.pallas_call(
        kernel,
        grid=(npb + nsb,),
        in_specs=[
            pl.BlockSpec((tm, d), lambda m: (m, 0)),
            pl.BlockSpec((tm, d), lambda m: (m, 1)),
            _resident((1, d), const2),
            _resident((1, d), const2),
            _resident((None, n_groups, SGU_CHUNK, SGU_CHUNK),
                      lambda m: (layer, 0, 0, 0)),
            _resident((2, SGU_CHUNK, n_groups), lambda m: (0, 0, 0)),
            pl.BlockSpec(memory_space=pl.ANY),
            pl.BlockSpec((tm, d), row_block),
            _resident((1, d), const2),
            _resident((1, d), const2),
        ],
        out_specs=[
            pl.BlockSpec((tm, d), row_block),
            pl.BlockSpec((tm, d), row_block),
            pl.BlockSpec((tm, d), lambda m: (jnp.maximum(m - npb, 0), 0)),
        ],
        out_shape=[
            jax.ShapeDtypeStruct((m_all, d), F32),
            jax.ShapeDtypeStruct((m_all, d), BF16),
            jax.ShapeDtypeStruct((ms, d), F32),
        ],
        scratch_shapes=[pltpu.VMEM((n_groups, SGU_CHUNK, SGU_CHUNK), BF16),
                        pltpu.VMEM((SGU_CHUNK, d), F32)]
        + _weight_scratch(d, d, tm),
        compiler_params=_params(1),
        name="sgu_out",
    )(zz, zz, ln_g, ln_b, w_s, bcol, w_out, x, gpost, gnext)


TM_CONV = 1024
TM_SAMPLE = 512
SUB_CONV = 256
TM_FFN = 2176
SUB_FFN = 272
SUB_SGU = 544
TM_TAIL = 512
SUB_TAIL = 128
TM_DOWN = 256
SUB_DOWN = 128
W_STAGE_ROWS = 512
TN_CONV = 512
TN_FFN = 512
TN_SGU = 1024


def kernel(x_prompt, x_sample, cache_conv, norm_mix_pre, norm_mix_post, norm_ffn_pre, norm_ffn_post, a_w_in, a_conv_w, a_w_out, b_w_in, b_b_in, b_ln_g, b_ln_b, b_w_s, b_b_s, b_w_out, ffn_w_gate, ffn_w_up, ffn_w_down):
    batch, seq, d = x_prompt.shape
    dec_batch, dec_seq, _ = x_sample.shape
    depth = norm_mix_pre.shape[0]
    mp, ms = batch * seq, dec_batch * dec_seq
    assert depth == 2 and seq % TM_CONV == 0 and ms % TM_SAMPLE == 0
    assert SGU_CHUNK % dec_seq == 0 and SUB_CONV % dec_seq == 0 and dec_seq >= CONV_W - 1
    assert mp % TM_TAIL == 0 and ms % TM_TAIL == 0
    assert seq % SGU_CHUNK == 0 and TM_DOWN % SGU_CHUNK == 0 and ms % TM_DOWN == 0
    assert (mp + ms) % TM_FFN == 0

    xp = x_prompt.reshape(mp, d)
    xs = x_sample.reshape(ms, d)
    row = lambda p, i: p[i].reshape(1, -1)

    conv_args = dict(layer=0, seq=seq, dec_seq=dec_seq, tn=TN_CONV, sub=SUB_CONV)
    p_p, tails = _conv_in(xp, row(norm_mix_pre, 0), a_w_in, a_conv_w[0], None,
                          tm=TM_CONV, **conv_args)
    hist = jnp.pad(cache_conv[0], ((0, 0), (0, dec_seq - (CONV_W - 1)), (0, 0)))
    p_s, cz_s = _conv_in(xs, row(norm_mix_pre, 0), a_w_in, a_conv_w[0],
                         hist.reshape(ms, d), tm=TM_SAMPLE, **conv_args)
    state_conv_prompt = tails[None, :, SUBLANES - (CONV_W - 1):, :]
    state_conv_sample = cz_s.reshape(dec_batch, dec_seq, d)[None, :, dec_seq - (CONV_W - 1):, :]
    x, h = _proj_out((p_p, p_s), a_w_out, (xp, xs), row(norm_mix_post, 0),
                     row(norm_ffn_pre, 0), layer=0, mp=mp, tm=TM_TAIL, sub=SUB_TAIL)

    a = _ffn_up(h, ffn_w_gate, ffn_w_up, layer=0, tm=TM_FFN, tn=TN_FFN, sub=SUB_FFN)
    x, h = _proj_out(a, ffn_w_down, x, row(norm_ffn_post, 0),
                     row(norm_mix_pre, 1), layer=0, mp=mp, tm=TM_DOWN, sub=SUB_DOWN)

    zz = _sgu_in(h, b_w_in, b_b_in[:, None, :], layer=0, tm=TM_FFN, tn=TN_SGU,
                 sub=SUB_SGU)
    reps = SGU_CHUNK // dec_seq
    b_s = b_b_s[0]
    b_rows = jnp.stack([b_s, jnp.tile(b_s[:, :dec_seq], (1, reps))])
    bcol = jnp.swapaxes(b_rows, 1, 2)
    x, h, vn_s = _sgu_out(zz, b_ln_g[0].reshape(1, -1), b_ln_b[0].reshape(1, -1),
                          b_w_s, bcol, b_w_out, x,
                          row(norm_mix_post, 1), row(norm_ffn_pre, 1),
                          layer=0, mp=mp, dec_seq=dec_seq, tm=TM_DOWN)
    state_sgu_v_sample = vn_s.reshape(1, dec_batch, dec_seq, d)

    a = _ffn_up(h, ffn_w_gate, ffn_w_up, layer=1, tm=TM_FFN, tn=TN_FFN, sub=SUB_FFN)
    yp, ys = _proj_out(a, ffn_w_down, x, row(norm_ffn_post, 1),
                       None, layer=1, mp=mp, tm=TM_DOWN, sub=SUB_DOWN)

    return (yp.reshape(batch, seq, d), ys.reshape(dec_batch, dec_seq, d),
            state_conv_prompt, state_conv_sample, state_sgu_v_sample)
```

```python
import functools
import math

import jax
import jax.numpy as jnp
from jax import lax
from jax.experimental import pallas as pl
from jax.experimental.pallas import tpu as pltpu

EPS = 1e-6
CONV_W = 3
SGU_CHUNK = 128
SGU_GROUP_DIM = 128
SUBLANES = 8
BF16_ROWS = 16

MIB = 1024 * 1024
VMEM_LIMIT_BYTES = 60 * MIB

F32 = jnp.float32
BF16 = jnp.bfloat16


def _params(n_axes):
    return pltpu.CompilerParams(
        dimension_semantics=("arbitrary",) * n_axes,
        vmem_limit_bytes=VMEM_LIMIT_BYTES)


def _rms(x, g):
    y = x * lax.rsqrt(jnp.mean(x * x, axis=-1, keepdims=True) + EPS)
    return y * g


def _dot(a, b):
    return jnp.dot(a, b, preferred_element_type=F32)


def _resident(block_shape, index_map):
    return pl.BlockSpec(block_shape, index_map, pipeline_mode=pl.Buffered(1))


def _round_specs(w, layer, n_inner, n_steps):
    k, d = w.shape[1:]
    assert k % n_steps == 0 and (k // n_steps) % BF16_ROWS == 0
    slab = k // n_steps
    step = lambda n, m: n * n_inner + m
    return (pl.BlockSpec((None, slab, d), lambda n, m: (layer, step(n, m), 0)),
            pl.BlockSpec((slab, d), lambda n, m: (step(n, m), 0)),
            jax.ShapeDtypeStruct((k, d), BF16))


def _conv_in_kernel(*refs, blocks_per_seq, dec_seq, sub, sample):
    if sample:
        (x_ref, g_ref, wb_ref, wc_ref, wz_ref, cw_ref, hist_ref,
         p_ref, state_ref, wbf_ref) = refs
    else:
        (x_ref, g_ref, wb_ref, wc_ref, wz_ref, cw_ref, wnext_ref,
         p_ref, state_ref, wnext_bf_ref, wbf_ref, carry_ref) = refs
        wnext_bf_ref[...] = wnext_ref[...].astype(BF16)
    m = pl.program_id(1)
    tm, tn = p_ref.shape

    @pl.when(m == 0)
    def _():
        wbf_ref[0] = wb_ref[...].astype(BF16)
        wbf_ref[1] = wc_ref[...].astype(BF16)
        wbf_ref[2] = wz_ref[...].astype(BF16)

    if not sample:
        @pl.when(m % blocks_per_seq == 0)
        def _():
            carry_ref[...] = jnp.zeros_like(carry_ref)

        prev = carry_ref[...]

    g = g_ref[...]
    cw = cw_ref[...]
    row = lax.broadcasted_iota(jnp.int32, (sub, tn), 0)
    for s in range(tm // sub):
        rows = pl.ds(s * sub, sub)
        h = _rms(x_ref[rows, :], g).astype(BF16)
        gate_b = _dot(h, wbf_ref[0])
        cz = _dot(h, wbf_ref[1]) * _dot(h, wbf_ref[2])

        if sample:
            hist = hist_ref[rows, :]
            pos = row % dec_seq
            cz1 = jnp.where(pos == 0, pltpu.roll(hist, sub - 1, 0), pltpu.roll(cz, 1, 0))
            cz2 = jnp.where(pos < CONV_W - 1, hist, pltpu.roll(cz, 2, 0))
            state_ref[rows, :] = cz
        else:
            prev1 = prev[SUBLANES - 1:SUBLANES, :]
            prev2 = prev[SUBLANES - 2:SUBLANES - 1, :]
            cz1 = jnp.where(row == 0, prev1, pltpu.roll(cz, 1, 0))
            cz2 = jnp.where(row == 0, prev2,
                            jnp.where(row == 1, prev1, pltpu.roll(cz, 2, 0)))
            prev = cz[sub - SUBLANES:, :]

        conv = cw[0:1, :] * cz2 + cw[1:2, :] * cz1 + cw[2:3, :] * cz
        p_ref[rows, :] = (gate_b * conv).astype(p_ref.dtype)

    if not sample:
        carry_ref[...] = prev
        state_ref[0] = prev


def _conv_in(x, g, w_in, conv_w, hist, w_next, *, layer, seq, dec_seq, tm, tn, sub):
    sample = hist is not None
    rows, d = x.shape
    nt = d // tn
    bps = seq // tm
    kernel = functools.partial(_conv_in_kernel, blocks_per_seq=bps,
                               dec_seq=dec_seq, sub=sub, sample=sample)
    row_tile = pl.BlockSpec((tm, tn), lambda n, m: (m, n))
    in_specs = [
        pl.BlockSpec((tm, d), lambda n, m: (m, 0)),
        pl.BlockSpec((1, d), lambda n, m: (0, 0)),
        pl.BlockSpec((None, d, tn), lambda n, m: (layer, 0, n)),
        pl.BlockSpec((None, d, tn), lambda n, m: (layer, 0, nt + n)),
        pl.BlockSpec((None, d, tn), lambda n, m: (layer, 0, 2 * nt + n)),
        pl.BlockSpec((CONV_W, tn), lambda n, m: (0, n)),
    ]
    scratch = [pltpu.VMEM((3, d, tn), BF16)]
    if sample:
        in_specs.append(row_tile)
        args = (x, g, w_in, w_in, w_in, conv_w, hist)
        state_spec = row_tile
        state_shape = jax.ShapeDtypeStruct((rows, d), F32)
    else:
        args = (x, g, w_in, w_in, w_in, conv_w, w_next)
        state_spec = pl.BlockSpec((1, SUBLANES, tn), lambda n, m: (m // bps, 0, n))
        state_shape = jax.ShapeDtypeStruct((rows // seq, SUBLANES, d), F32)
        scratch.append(pltpu.VMEM((SUBLANES, tn), F32))
        w_in_spec, w_out_spec, w_shape = _round_specs(w_next, layer, rows // tm,
                                                      nt * (rows // tm))
        in_specs.append(w_in_spec)
    out_specs = [row_tile, state_spec] + ([] if sample else [w_out_spec])
    out_shape = [jax.ShapeDtypeStruct((rows, d), BF16), state_shape]
    out_shape += [] if sample else [w_shape]
    return pl.pallas_call(
        kernel,
        grid=(nt, rows // tm),
        in_specs=in_specs,
        out_specs=out_specs,
        out_shape=out_shape,
        scratch_shapes=scratch,
        compiler_params=_params(2),
        name="conv_in_sample" if sample else "conv_in_prompt",
    )(*args)


def _proj_out_kernel(*refs, n_prompt_blocks, split_in, final, sub):
    m = pl.program_id(0)
    refs = list(refs)
    if split_in:
        ap_ref, as_ref, wbf_ref, xp_ref, xs_ref = refs[:5]
        refs = refs[5:]
    else:
        a_ref, wbf_ref, x_ref = refs[:3]
        refs = refs[3:]
    gpost_ref = refs[0]
    refs = refs[1:]
    if not final:
        gnext_ref = refs[0]
        refs = refs[1:]
    o0_ref, o1_ref = refs

    is_prompt = m < n_prompt_blocks

    def run(a_ref, x_ref, xo_ref, ho_ref):
        for s in range(a_ref.shape[0] // sub):
            rows = pl.ds(s * sub, sub)
            y = _dot(a_ref[rows, :], wbf_ref[...])
            xn = x_ref[rows, :] + _rms(y, gpost_ref[...])
            xo_ref[rows, :] = xn
            if ho_ref is not None:
                ho_ref[rows, :] = _rms(xn, gnext_ref[...]).astype(ho_ref.dtype)

    if split_in:
        pl.when(is_prompt)(lambda: run(ap_ref, xp_ref, o0_ref, o1_ref))
        pl.when(jnp.logical_not(is_prompt))(lambda: run(as_ref, xs_ref, o0_ref, o1_ref))
    elif final:
        pl.when(is_prompt)(lambda: run(a_ref, x_ref, o0_ref, None))
        pl.when(jnp.logical_not(is_prompt))(lambda: run(a_ref, x_ref, o1_ref, None))
    else:
        run(a_ref, x_ref, o0_ref, o1_ref)


def _proj_out(a, w, res, gpost, gnext, *, mp, tm, sub):
    final = gnext is None
    split_in = isinstance(a, tuple)
    assert split_in == isinstance(res, tuple)
    k, d = w.shape
    m_all = sum(t.shape[0] for t in a) if split_in else a.shape[0]
    ms = m_all - mp
    npb, nsb = mp // tm, ms // tm
    prompt_block = lambda m: (jnp.minimum(m, npb - 1), 0)
    sample_block = lambda m: (jnp.maximum(m - npb, 0), 0)
    row_block = lambda m: (m, 0)
    w_spec = _resident((k, d), lambda m: (0, 0))
    if split_in:
        in_specs = [pl.BlockSpec((tm, k), prompt_block), pl.BlockSpec((tm, k), sample_block),
                    w_spec,
                    pl.BlockSpec((tm, d), prompt_block), pl.BlockSpec((tm, d), sample_block)]
        args = (*a, w, *res)
    else:
        in_specs = [pl.BlockSpec((tm, k), row_block), w_spec,
                    pl.BlockSpec((tm, d), row_block)]
        args = (a, w, res)
    norm_args = (gpost,) if final else (gpost, gnext)
    in_specs += [_resident((1, d), lambda m: (0, 0)) for _ in norm_args]
    if final:
        out_specs = [pl.BlockSpec((tm, d), prompt_block),
                     pl.BlockSpec((tm, d), sample_block)]
        out_shape = [jax.ShapeDtypeStruct((mp, d), F32),
                     jax.ShapeDtypeStruct((ms, d), F32)]
    else:
        out_specs = [pl.BlockSpec((tm, d), row_block),
                     pl.BlockSpec((tm, d), row_block)]
        out_shape = [jax.ShapeDtypeStruct((m_all, d), F32),
                     jax.ShapeDtypeStruct((m_all, d), BF16)]
    kernel = functools.partial(_proj_out_kernel, n_prompt_blocks=npb,
                               split_in=split_in, final=final, sub=sub)
    return pl.pallas_call(
        kernel,
        grid=(npb + nsb,),
        in_specs=in_specs,
        out_specs=out_specs,
        out_shape=out_shape,
        compiler_params=_params(1),
        name="proj_out_k%d" % k,
    )(*args, *norm_args)


def _ffn_up_kernel(h_ref, wg_ref, wu_ref, wnext_ref, a_ref, wnext_bf_ref, wbf_ref, *, sub):
    wnext_bf_ref[...] = wnext_ref[...].astype(BF16)

    @pl.when(pl.program_id(1) == 0)
    def _():
        wbf_ref[0] = wg_ref[...].astype(BF16)
        wbf_ref[1] = wu_ref[...].astype(BF16)

    for s in range(a_ref.shape[0] // sub):
        rows = pl.ds(s * sub, sub)
        h = h_ref[rows, :]
        g = _dot(h, wbf_ref[0])
        u = _dot(h, wbf_ref[1])
        a_ref[rows, :] = (g * jax.nn.sigmoid(g) * u).astype(a_ref.dtype)


def _ffn_up(h, w_gate, w_up, w_next, *, layer, tm, tn, sub):
    m_all, d = h.shape
    f = w_gate.shape[2]
    grid = (f // tn, m_all // tm)
    w_in_spec, w_out_spec, w_shape = _round_specs(w_next, layer, grid[1], grid[0] * grid[1])
    return pl.pallas_call(
        functools.partial(_ffn_up_kernel, sub=sub),
        grid=grid,
        in_specs=[
            pl.BlockSpec((tm, d), lambda n, m: (m, 0)),
            pl.BlockSpec((None, d, tn), lambda n, m: (layer, 0, n)),
            pl.BlockSpec((None, d, tn), lambda n, m: (layer, 0, n)),
            w_in_spec,
        ],
        out_specs=[pl.BlockSpec((tm, tn), lambda n, m: (m, n)), w_out_spec],
        out_shape=[jax.ShapeDtypeStruct((m_all, f), BF16), w_shape],
        scratch_shapes=[pltpu.VMEM((2, d, tn), BF16)],
        compiler_params=_params(2),
        name="ffn_up",
    )(h, w_gate, w_up, w_next)


def _sgu_in_kernel(h_ref, w_ref, b_ref, wnext_ref, zz_ref, wnext_bf_ref, wbf_ref, *, sub):
    wnext_bf_ref[...] = wnext_ref[...].astype(BF16)

    @pl.when(pl.program_id(1) == 0)
    def _():
        wbf_ref[...] = w_ref[...].astype(BF16)

    for s in range(zz_ref.shape[0] // sub):
        rows = pl.ds(s * sub, sub)
        z = _dot(h_ref[rows, :], wbf_ref[...]) + b_ref[...]
        gelu = 0.5 * z * (1.0 + lax.erf(z * math.sqrt(0.5)))
        zz_ref[rows, :] = gelu.astype(zz_ref.dtype)


def _sgu_in(h, w, b, w_next, *, layer, tm, tn, sub):
    m_all, d = h.shape
    n_out = w.shape[2]
    grid = (n_out // tn, m_all // tm)
    w_in_spec, w_out_spec, w_shape = _round_specs(w_next, layer, grid[1], grid[0] * grid[1])
    return pl.pallas_call(
        functools.partial(_sgu_in_kernel, sub=sub),
        grid=grid,
        in_specs=[
            pl.BlockSpec((tm, d), lambda n, m: (m, 0)),
            pl.BlockSpec((None, d, tn), lambda n, m: (layer, 0, n)),
            pl.BlockSpec((None, 1, tn), lambda n, m: (layer, 0, n)),
            w_in_spec,
        ],
        out_specs=[pl.BlockSpec((tm, tn), lambda n, m: (m, n)), w_out_spec],
        out_shape=[jax.ShapeDtypeStruct((m_all, n_out), BF16), w_shape],
        scratch_shapes=[pltpu.VMEM((d, tn), BF16)],
        compiler_params=_params(2),
        name="sgu_in",
    )(h, w, b, w_next)


def _sgu_out_kernel(u_ref, v_ref, lng_ref, lnb_ref, ws_ref, bcol_ref, wbf_ref,
                    x_ref, gpost_ref, gnext_ref,
                    xo_ref, ho_ref, vn_ref,
                    mixw_ref, mixb_ref,
                    *, n_prompt_blocks, dec_seq):
    m = pl.program_id(0)
    tm, d = x_ref.shape
    n_groups = d // SGU_GROUP_DIM

    def build_mixing(sample_kind):
        i = lax.broadcasted_iota(jnp.int32, (SGU_CHUNK, SGU_CHUNK), 0)
        j = lax.broadcasted_iota(jnp.int32, (SGU_CHUNK, SGU_CHUNK), 1)
        mask = i >= j
        if sample_kind:
            mask = mask & ((i // dec_seq) == (j // dec_seq))
        reps = SGU_CHUNK // dec_seq
        bcol = bcol_ref[int(sample_kind)]
        for g in range(n_groups):
            cols = slice(g * SGU_GROUP_DIM, (g + 1) * SGU_GROUP_DIM)
            if sample_kind:
                w = ws_ref[g, :dec_seq, :dec_seq]
                w = jnp.concatenate([w] * reps, axis=0)
                w = jnp.concatenate([w] * reps, axis=1)
            else:
                w = ws_ref[g]
            mixw_ref[g] = jnp.where(mask, w, 0.0).astype(BF16)
            mixb_ref[:, cols] = jnp.broadcast_to(bcol[:, g:g + 1],
                                                 (SGU_CHUNK, SGU_GROUP_DIM))

    pl.when(m == 0)(lambda: build_mixing(False))
    pl.when(m == n_prompt_blocks)(lambda: build_mixing(True))

    for c in range(tm // SGU_CHUNK):
        rows = pl.ds(c * SGU_CHUNK, SGU_CHUNK)
        v = v_ref[rows, :].astype(F32)
        mu = jnp.mean(v, axis=-1, keepdims=True)
        vc = v - mu
        vn = vc * lax.rsqrt(jnp.mean(vc * vc, axis=-1, keepdims=True) + EPS)
        vn = vn * lng_ref[...] + lnb_ref[...]
        vn_ref[rows, :] = vn
        vnb = vn.astype(BF16)

        gated = []
        for g in range(n_groups):
            cols = slice(g * SGU_GROUP_DIM, (g + 1) * SGU_GROUP_DIM)
            mixed = _dot(mixw_ref[g], vnb[:, cols]) + mixb_ref[:, cols]
            gated.append((u_ref[rows, cols].astype(F32) * mixed).astype(BF16))
        y = _dot(jnp.concatenate(gated, axis=1), wbf_ref[...])

        xn = x_ref[rows, :] + _rms(y, gpost_ref[...])
        xo_ref[rows, :] = xn
        ho_ref[rows, :] = _rms(xn, gnext_ref[...]).astype(ho_ref.dtype)


def _sgu_out(zz, ln_g, ln_b, w_s, bcol, w_out, x, gpost, gnext,
             *, layer, mp, dec_seq, tm):
    m_all, d = x.shape
    ms = m_all - mp
    npb, nsb = mp // tm, ms // tm
    n_groups = d // SGU_GROUP_DIM
    row_block = lambda m: (m, 0)
    const2 = lambda m: (0, 0)
    kernel = functools.partial(_sgu_out_kernel, n_prompt_blocks=npb, dec_seq=dec_seq)
    return pl.pallas_call(
        kernel,
        grid=(npb + nsb,),
        in_specs=[
            pl.BlockSpec((tm, d), lambda m: (m, 0)),
            pl.BlockSpec((tm, d), lambda m: (m, 1)),
            _resident((1, d), const2),
            _resident((1, d), const2),
            _resident((None, n_groups, SGU_CHUNK, SGU_CHUNK),
                      lambda m: (layer, 0, 0, 0)),
            _resident((2, SGU_CHUNK, n_groups), lambda m: (0, 0, 0)),
            _resident((d, d), const2),
            pl.BlockSpec((tm, d), row_block),
            _resident((1, d), const2),
            _resident((1, d), const2),
        ],
        out_specs=[
            pl.BlockSpec((tm, d), row_block),
            pl.BlockSpec((tm, d), row_block),
            pl.BlockSpec((tm, d), lambda m: (jnp.maximum(m - npb, 0), 0)),
        ],
        out_shape=[
            jax.ShapeDtypeStruct((m_all, d), F32),
            jax.ShapeDtypeStruct((m_all, d), BF16),
            jax.ShapeDtypeStruct((ms, d), F32),
        ],
        scratch_shapes=[pltpu.VMEM((n_groups, SGU_CHUNK, SGU_CHUNK), BF16),
                        pltpu.VMEM((SGU_CHUNK, d), F32)],
        compiler_params=_params(1),
        name="sgu_out",
    )(zz, zz, ln_g, ln_b, w_s, bcol, w_out, x, gpost, gnext)


TM_CONV = 1024
TM_SAMPLE = 512
SUB_CONV = 256
TM_FFN = 2176
SUB_FFN = 272
SUB_SGU = 544
TM_TAIL = 512
SUB_TAIL = 128
TM_DOWN = 512
SUB_DOWN = 128
TN_CONV = 512
TN_FFN = 512
TN_SGU = 1024


def kernel(x_prompt, x_sample, cache_conv, norm_mix_pre, norm_mix_post, norm_ffn_pre, norm_ffn_post, a_w_in, a_conv_w, a_w_out, b_w_in, b_b_in, b_ln_g, b_ln_b, b_w_s, b_b_s, b_w_out, ffn_w_gate, ffn_w_up, ffn_w_down):
    batch, seq, d = x_prompt.shape
    dec_batch, dec_seq, _ = x_sample.shape
    depth = norm_mix_pre.shape[0]
    mp, ms = batch * seq, dec_batch * dec_seq
    assert depth == 2 and seq % TM_CONV == 0 and ms % TM_SAMPLE == 0
    assert SGU_CHUNK % dec_seq == 0 and SUB_CONV % dec_seq == 0 and dec_seq >= CONV_W - 1
    assert mp % TM_TAIL == 0 and ms % TM_TAIL == 0
    assert seq % SGU_CHUNK == 0 and TM_DOWN % SGU_CHUNK == 0 and ms % TM_DOWN == 0
    assert (mp + ms) % TM_FFN == 0

    xp = x_prompt.reshape(mp, d)
    xs = x_sample.reshape(ms, d)
    row = lambda p, i: p[i].reshape(1, -1)

    conv_args = dict(layer=0, seq=seq, dec_seq=dec_seq, tn=TN_CONV, sub=SUB_CONV)
    p_p, tails, w_out_bf = _conv_in(xp, row(norm_mix_pre, 0), a_w_in, a_conv_w[0], None,
                                    a_w_out, tm=TM_CONV, **conv_args)
    hist = jnp.pad(cache_conv[0], ((0, 0), (0, dec_seq - (CONV_W - 1)), (0, 0)))
    p_s, cz_s = _conv_in(xs, row(norm_mix_pre, 0), a_w_in, a_conv_w[0],
                         hist.reshape(ms, d), None, tm=TM_SAMPLE, **conv_args)
    state_conv_prompt = tails[None, :, SUBLANES - (CONV_W - 1):, :]
    state_conv_sample = cz_s.reshape(dec_batch, dec_seq, d)[None, :, dec_seq - (CONV_W - 1):, :]
    x, h = _proj_out((p_p, p_s), w_out_bf, (xp, xs), row(norm_mix_post, 0),
                     row(norm_ffn_pre, 0), mp=mp, tm=TM_TAIL, sub=SUB_TAIL)

    a, w_down_bf = _ffn_up(h, ffn_w_gate, ffn_w_up, ffn_w_down, layer=0,
                           tm=TM_FFN, tn=TN_FFN, sub=SUB_FFN)
    x, h = _proj_out(a, w_down_bf, x, row(norm_ffn_post, 0),
                     row(norm_mix_pre, 1), mp=mp, tm=TM_DOWN, sub=SUB_DOWN)

    zz, w_out_bf = _sgu_in(h, b_w_in, b_b_in[:, None, :], b_w_out, layer=0,
                           tm=TM_FFN, tn=TN_SGU, sub=SUB_SGU)
    reps = SGU_CHUNK // dec_seq
    b_s = b_b_s[0]
    b_rows = jnp.stack([b_s, jnp.tile(b_s[:, :dec_seq], (1, reps))])
    bcol = jnp.swapaxes(b_rows, 1, 2)
    x, h, vn_s = _sgu_out(zz, b_ln_g[0].reshape(1, -1), b_ln_b[0].reshape(1, -1),
                          b_w_s, bcol, w_out_bf, x,
                          row(norm_mix_post, 1), row(norm_ffn_pre, 1),
                          layer=0, mp=mp, dec_seq=dec_seq, tm=TM_DOWN)
    state_sgu_v_sample = vn_s.reshape(1, dec_batch, dec_seq, d)

    a, w_down_bf = _ffn_up(h, ffn_w_gate, ffn_w_up, ffn_w_down, layer=1,
                           tm=TM_FFN, tn=TN_FFN, sub=SUB_FFN)
    yp, ys = _proj_out(a, w_down_bf, x, row(norm_ffn_post, 1),
                       None, mp=mp, tm=TM_DOWN, sub=SUB_DOWN)

    return (yp.reshape(batch, seq, d), ys.reshape(dec_batch, dec_seq, d),
            state_conv_prompt, state_conv_sample, state_sgu_v_sample)
```

```python
import functools
import math

import jax
import jax.numpy as jnp
from jax import lax
from jax.experimental import pallas as pl
from jax.experimental.pallas import tpu as pltpu

EPS = 1e-6
CONV_W = 3
SGU_CHUNK = 128
SGU_GROUP_DIM = 128
SUBLANES = 8
BF16_ROWS = 16

MIB = 1024 * 1024
VMEM_LIMIT_BYTES = 60 * MIB

F32 = jnp.float32
BF16 = jnp.bfloat16


def _params(n_axes):
    return pltpu.CompilerParams(
        dimension_semantics=("arbitrary",) * n_axes,
        vmem_limit_bytes=VMEM_LIMIT_BYTES)


def _rms(x, g):
    y = x * lax.rsqrt(jnp.mean(x * x, axis=-1, keepdims=True) + EPS)
    return y * g


def _dot(a, b):
    return jnp.dot(a, b, preferred_element_type=F32)


def _resident(block_shape, index_map):
    return pl.BlockSpec(block_shape, index_map, pipeline_mode=pl.Buffered(1))


def _round_specs(w, layer, n_inner, n_steps):
    k, d = w.shape[1:]
    assert k % n_steps == 0 and (k // n_steps) % BF16_ROWS == 0
    slab = k // n_steps
    step = lambda n, m: n * n_inner + jnp.minimum(m, n_inner - 1)
    return (pl.BlockSpec((None, slab, d), lambda n, m: (layer, step(n, m), 0)),
            pl.BlockSpec((slab, d), lambda n, m: (step(n, m), 0)),
            jax.ShapeDtypeStruct((k, d), BF16))


def _conv_in_kernel(xp_ref, xs_ref, g_ref, wb_ref, wc_ref, wz_ref, cw_ref, hist_ref,
                    wnext_ref,
                    pp_ref, ps_ref, tails_ref, czs_ref, wnext_bf_ref,
                    wbf_ref, carry_ref,
                    *, n_prompt_blocks, blocks_per_seq, dec_seq, sub):
    m = pl.program_id(1)
    tn = pp_ref.shape[1]
    is_prompt = m < n_prompt_blocks
    wnext_bf_ref[...] = wnext_ref[...].astype(BF16)

    @pl.when(m == 0)
    def _():
        wbf_ref[0] = wb_ref[...].astype(BF16)
        wbf_ref[1] = wc_ref[...].astype(BF16)
        wbf_ref[2] = wz_ref[...].astype(BF16)

    @pl.when(is_prompt & (m % blocks_per_seq == 0))
    def _():
        carry_ref[...] = jnp.zeros_like(carry_ref)

    g = g_ref[...]
    cw = cw_ref[...]
    row = lax.broadcasted_iota(jnp.int32, (sub, tn), 0)

    def project(x_ref, rows):
        h = _rms(x_ref[rows, :], g).astype(BF16)
        gate_b = _dot(h, wbf_ref[0])
        return gate_b, _dot(h, wbf_ref[1]) * _dot(h, wbf_ref[2])

    def gated_conv(gate_b, cz, cz1, cz2):
        conv = cw[0:1, :] * cz2 + cw[1:2, :] * cz1 + cw[2:3, :] * cz
        return (gate_b * conv).astype(BF16)

    @pl.when(is_prompt)
    def _():
        prev = carry_ref[...]
        for s in range(pp_ref.shape[0] // sub):
            rows = pl.ds(s * sub, sub)
            gate_b, cz = project(xp_ref, rows)
            prev1 = prev[SUBLANES - 1:SUBLANES, :]
            prev2 = prev[SUBLANES - 2:SUBLANES - 1, :]
            cz1 = jnp.where(row == 0, prev1, pltpu.roll(cz, 1, 0))
            cz2 = jnp.where(row == 0, prev2,
                            jnp.where(row == 1, prev1, pltpu.roll(cz, 2, 0)))
            pp_ref[rows, :] = gated_conv(gate_b, cz, cz1, cz2)
            prev = cz[sub - SUBLANES:, :]
        carry_ref[...] = prev
        tails_ref[0] = prev

    @pl.when(jnp.logical_not(is_prompt))
    def _():
        pos = row % dec_seq
        for s in range(ps_ref.shape[0] // sub):
            rows = pl.ds(s * sub, sub)
            gate_b, cz = project(xs_ref, rows)
            hist = hist_ref[rows, :]
            cz1 = jnp.where(pos == 0, pltpu.roll(hist, sub - 1, 0), pltpu.roll(cz, 1, 0))
            cz2 = jnp.where(pos < CONV_W - 1, hist, pltpu.roll(cz, 2, 0))
            ps_ref[rows, :] = gated_conv(gate_b, cz, cz1, cz2)
            czs_ref[rows, :] = cz


def _conv_in(xp, xs, g, w_in, conv_w, hist, w_next, *, layer, seq, dec_seq, tm, tn, sub):
    (mp, d), ms = xp.shape, xs.shape[0]
    nt = d // tn
    npb, bps = mp // tm, seq // tm
    kernel = functools.partial(_conv_in_kernel, n_prompt_blocks=npb, blocks_per_seq=bps,
                               dec_seq=dec_seq, sub=sub)
    prompt_rows = lambda n, m: jnp.minimum(m, npb - 1)
    prompt_tile = pl.BlockSpec((tm, tn), lambda n, m: (prompt_rows(n, m), n))
    sample_tile = pl.BlockSpec((ms, tn), lambda n, m: (0, n))
    w_in_spec, w_out_spec, w_shape = _round_specs(w_next, layer, npb, nt * npb)
    return pl.pallas_call(
        kernel,
        grid=(nt, npb + 1),
        in_specs=[
            pl.BlockSpec((tm, d), lambda n, m: (prompt_rows(n, m), 0)),
            _resident((ms, d), lambda n, m: (0, 0)),
            _resident((1, d), lambda n, m: (0, 0)),
            pl.BlockSpec((None, d, tn), lambda n, m: (layer, 0, n)),
            pl.BlockSpec((None, d, tn), lambda n, m: (layer, 0, nt + n)),
            pl.BlockSpec((None, d, tn), lambda n, m: (layer, 0, 2 * nt + n)),
            pl.BlockSpec((CONV_W, tn), lambda n, m: (0, n)),
            pl.BlockSpec((ms, tn), lambda n, m: (0, n), pipeline_mode=pl.Buffered(1)),
            w_in_spec,
        ],
        out_specs=[
            prompt_tile,
            sample_tile,
            pl.BlockSpec((1, SUBLANES, tn), lambda n, m: (prompt_rows(n, m) // bps, 0, n)),
            sample_tile,
            w_out_spec,
        ],
        out_shape=[
            jax.ShapeDtypeStruct((mp, d), BF16),
            jax.ShapeDtypeStruct((ms, d), BF16),
            jax.ShapeDtypeStruct((mp // seq, SUBLANES, d), F32),
            jax.ShapeDtypeStruct((ms, d), F32),
            w_shape,
        ],
        scratch_shapes=[pltpu.VMEM((3, d, tn), BF16), pltpu.VMEM((SUBLANES, tn), F32)],
        compiler_params=_params(2),
        name="conv_in",
    )(xp, xs, g, w_in, w_in, w_in, conv_w, hist, w_next)


def _proj_out_kernel(*refs, n_prompt_blocks, split_in, final, sub):
    m = pl.program_id(0)
    refs = list(refs)
    if split_in:
        ap_ref, as_ref, wbf_ref, xp_ref, xs_ref = refs[:5]
        refs = refs[5:]
    else:
        a_ref, wbf_ref, x_ref = refs[:3]
        refs = refs[3:]
    gpost_ref = refs[0]
    refs = refs[1:]
    if not final:
        gnext_ref = refs[0]
        refs = refs[1:]
    o0_ref, o1_ref = refs

    is_prompt = m < n_prompt_blocks

    def run(a_ref, x_ref, xo_ref, ho_ref):
        for s in range(a_ref.shape[0] // sub):
            rows = pl.ds(s * sub, sub)
            y = _dot(a_ref[rows, :], wbf_ref[...])
            xn = x_ref[rows, :] + _rms(y, gpost_ref[...])
            xo_ref[rows, :] = xn
            if ho_ref is not None:
                ho_ref[rows, :] = _rms(xn, gnext_ref[...]).astype(ho_ref.dtype)

    if split_in:
        pl.when(is_prompt)(lambda: run(ap_ref, xp_ref, o0_ref, o1_ref))
        pl.when(jnp.logical_not(is_prompt))(lambda: run(as_ref, xs_ref, o0_ref, o1_ref))
    elif final:
        pl.when(is_prompt)(lambda: run(a_ref, x_ref, o0_ref, None))
        pl.when(jnp.logical_not(is_prompt))(lambda: run(a_ref, x_ref, o1_ref, None))
    else:
        run(a_ref, x_ref, o0_ref, o1_ref)


def _proj_out(a, w, res, gpost, gnext, *, mp, tm, sub):
    final = gnext is None
    split_in = isinstance(a, tuple)
    assert split_in == isinstance(res, tuple)
    k, d = w.shape
    m_all = sum(t.shape[0] for t in a) if split_in else a.shape[0]
    ms = m_all - mp
    npb, nsb = mp // tm, ms // tm
    prompt_block = lambda m: (jnp.minimum(m, npb - 1), 0)
    sample_block = lambda m: (jnp.maximum(m - npb, 0), 0)
    row_block = lambda m: (m, 0)
    w_spec = _resident((k, d), lambda m: (0, 0))
    if split_in:
        in_specs = [pl.BlockSpec((tm, k), prompt_block), pl.BlockSpec((tm, k), sample_block),
                    w_spec,
                    pl.BlockSpec((tm, d), prompt_block), pl.BlockSpec((tm, d), sample_block)]
        args = (*a, w, *res)
    else:
        in_specs = [pl.BlockSpec((tm, k), row_block), w_spec,
                    pl.BlockSpec((tm, d), row_block)]
        args = (a, w, res)
    norm_args = (gpost,) if final else (gpost, gnext)
    in_specs += [_resident((1, d), lambda m: (0, 0)) for _ in norm_args]
    if final:
        out_specs = [pl.BlockSpec((tm, d), prompt_block),
                     pl.BlockSpec((tm, d), sample_block)]
        out_shape = [jax.ShapeDtypeStruct((mp, d), F32),
                     jax.ShapeDtypeStruct((ms, d), F32)]
    else:
        out_specs = [pl.BlockSpec((tm, d), row_block),
                     pl.BlockSpec((tm, d), row_block)]
        out_shape = [jax.ShapeDtypeStruct((m_all, d), F32),
                     jax.ShapeDtypeStruct((m_all, d), BF16)]
    kernel = functools.partial(_proj_out_kernel, n_prompt_blocks=npb,
                               split_in=split_in, final=final, sub=sub)
    return pl.pallas_call(
        kernel,
        grid=(npb + nsb,),
        in_specs=in_specs,
        out_specs=out_specs,
        out_shape=out_shape,
        compiler_params=_params(1),
        name="proj_out_k%d" % k,
    )(*args, *norm_args)


def _ffn_up_kernel(h_ref, wg_ref, wu_ref, wnext_ref, a_ref, wnext_bf_ref, wbf_ref, *, sub):
    wnext_bf_ref[...] = wnext_ref[...].astype(BF16)

    @pl.when(pl.program_id(1) == 0)
    def _():
        wbf_ref[0] = wg_ref[...].astype(BF16)
        wbf_ref[1] = wu_ref[...].astype(BF16)

    for s in range(a_ref.shape[0] // sub):
        rows = pl.ds(s * sub, sub)
        h = h_ref[rows, :]
        g = _dot(h, wbf_ref[0])
        u = _dot(h, wbf_ref[1])
        a_ref[rows, :] = (g * jax.nn.sigmoid(g) * u).astype(a_ref.dtype)


def _ffn_up(h, w_gate, w_up, w_next, *, layer, tm, tn, sub):
    m_all, d = h.shape
    f = w_gate.shape[2]
    grid = (f // tn, m_all // tm)
    w_in_spec, w_out_spec, w_shape = _round_specs(w_next, layer, grid[1], grid[0] * grid[1])
    return pl.pallas_call(
        functools.partial(_ffn_up_kernel, sub=sub),
        grid=grid,
        in_specs=[
            pl.BlockSpec((tm, d), lambda n, m: (m, 0)),
            pl.BlockSpec((None, d, tn), lambda n, m: (layer, 0, n)),
            pl.BlockSpec((None, d, tn), lambda n, m: (layer, 0, n)),
            w_in_spec,
        ],
        out_specs=[pl.BlockSpec((tm, tn), lambda n, m: (m, n)), w_out_spec],
        out_shape=[jax.ShapeDtypeStruct((m_all, f), BF16), w_shape],
        scratch_shapes=[pltpu.VMEM((2, d, tn), BF16)],
        compiler_params=_params(2),
        name="ffn_up",
    )(h, w_gate, w_up, w_next)


def _sgu_in_kernel(h_ref, w_ref, b_ref, wnext_ref, zz_ref, wnext_bf_ref, wbf_ref, *, sub):
    wnext_bf_ref[...] = wnext_ref[...].astype(BF16)

    @pl.when(pl.program_id(1) == 0)
    def _():
        wbf_ref[...] = w_ref[...].astype(BF16)

    for s in range(zz_ref.shape[0] // sub):
        rows = pl.ds(s * sub, sub)
        z = _dot(h_ref[rows, :], wbf_ref[...]) + b_ref[...]
        gelu = 0.5 * z * (1.0 + lax.erf(z * math.sqrt(0.5)))
        zz_ref[rows, :] = gelu.astype(zz_ref.dtype)


def _sgu_in(h, w, b, w_next, *, layer, tm, tn, sub):
    m_all, d = h.shape
    n_out = w.shape[2]
    grid = (n_out // tn, m_all // tm)
    w_in_spec, w_out_spec, w_shape = _round_specs(w_next, layer, grid[1], grid[0] * grid[1])
    return pl.pallas_call(
        functools.partial(_sgu_in_kernel, sub=sub),
        grid=grid,
        in_specs=[
            pl.BlockSpec((tm, d), lambda n, m: (m, 0)),
            pl.BlockSpec((None, d, tn), lambda n, m: (layer, 0, n)),
            pl.BlockSpec((None, 1, tn), lambda n, m: (layer, 0, n)),
            w_in_spec,
        ],
        out_specs=[pl.BlockSpec((tm, tn), lambda n, m: (m, n)), w_out_spec],
        out_shape=[jax.ShapeDtypeStruct((m_all, n_out), BF16), w_shape],
        scratch_shapes=[pltpu.VMEM((d, tn), BF16)],
        compiler_params=_params(2),
        name="sgu_in",
    )(h, w, b, w_next)


def _sgu_out_kernel(u_ref, v_ref, lng_ref, lnb_ref, ws_ref, bcol_ref, wbf_ref,
                    x_ref, gpost_ref, gnext_ref,
                    xo_ref, ho_ref, vn_ref,
                    mixw_ref, mixb_ref,
                    *, n_prompt_blocks, dec_seq):
    m = pl.program_id(0)
    tm, d = x_ref.shape
    n_groups = d // SGU_GROUP_DIM

    def build_mixing(sample_kind):
        i = lax.broadcasted_iota(jnp.int32, (SGU_CHUNK, SGU_CHUNK), 0)
        j = lax.broadcasted_iota(jnp.int32, (SGU_CHUNK, SGU_CHUNK), 1)
        mask = i >= j
        if sample_kind:
            mask = mask & ((i // dec_seq) == (j // dec_seq))
        reps = SGU_CHUNK // dec_seq
        bcol = bcol_ref[int(sample_kind)]
        for g in range(n_groups):
            cols = slice(g * SGU_GROUP_DIM, (g + 1) * SGU_GROUP_DIM)
            if sample_kind:
                w = ws_ref[g, :dec_seq, :dec_seq]
                w = jnp.concatenate([w] * reps, axis=0)
                w = jnp.concatenate([w] * reps, axis=1)
            else:
                w = ws_ref[g]
            mixw_ref[g] = jnp.where(mask, w, 0.0).astype(BF16)
            mixb_ref[:, cols] = jnp.broadcast_to(bcol[:, g:g + 1],
                                                 (SGU_CHUNK, SGU_GROUP_DIM))

    pl.when(m == 0)(lambda: build_mixing(False))
    pl.when(m == n_prompt_blocks)(lambda: build_mixing(True))

    for c in range(tm // SGU_CHUNK):
        rows = pl.ds(c * SGU_CHUNK, SGU_CHUNK)
        v = v_ref[rows, :].astype(F32)
        mu = jnp.mean(v, axis=-1, keepdims=True)
        vc = v - mu
        vn = vc * lax.rsqrt(jnp.mean(vc * vc, axis=-1, keepdims=True) + EPS)
        vn = vn * lng_ref[...] + lnb_ref[...]
        vn_ref[rows, :] = vn
        vnb = vn.astype(BF16)

        gated = []
        for g in range(n_groups):
            cols = slice(g * SGU_GROUP_DIM, (g + 1) * SGU_GROUP_DIM)
            mixed = _dot(mixw_ref[g], vnb[:, cols]) + mixb_ref[:, cols]
            gated.append((u_ref[rows, cols].astype(F32) * mixed).astype(BF16))
        y = _dot(jnp.concatenate(gated, axis=1), wbf_ref[...])

        xn = x_ref[rows, :] + _rms(y, gpost_ref[...])
        xo_ref[rows, :] = xn
        ho_ref[rows, :] = _rms(xn, gnext_ref[...]).astype(ho_ref.dtype)


def _sgu_out(zz, ln_g, ln_b, w_s, bcol, w_out, x, gpost, gnext,
             *, layer, mp, dec_seq, tm):
    m_all, d = x.shape
    ms = m_all - mp
    npb, nsb = mp // tm, ms // tm
    n_groups = d // SGU_GROUP_DIM
    row_block = lambda m: (m, 0)
    const2 = lambda m: (0, 0)
    kernel = functools.partial(_sgu_out_kernel, n_prompt_blocks=npb, dec_seq=dec_seq)
    return pl.pallas_call(
        kernel,
        grid=(npb + nsb,),
        in_specs=[
            pl.BlockSpec((tm, d), lambda m: (m, 0)),
            pl.BlockSpec((tm, d), lambda m: (m, 1)),
            _resident((1, d), const2),
            _resident((1, d), const2),
            _resident((None, n_groups, SGU_CHUNK, SGU_CHUNK),
                      lambda m: (layer, 0, 0, 0)),
            _resident((2, SGU_CHUNK, n_groups), lambda m: (0, 0, 0)),
            _resident((d, d), const2),
            pl.BlockSpec((tm, d), row_block),
            _resident((1, d), const2),
            _resident((1, d), const2),
        ],
        out_specs=[
            pl.BlockSpec((tm, d), row_block),
            pl.BlockSpec((tm, d), row_block),
            pl.BlockSpec((tm, d), lambda m: (jnp.maximum(m - npb, 0), 0)),
        ],
        out_shape=[
            jax.ShapeDtypeStruct((m_all, d), F32),
            jax.ShapeDtypeStruct((m_all, d), BF16),
            jax.ShapeDtypeStruct((ms, d), F32),
        ],
        scratch_shapes=[pltpu.VMEM((n_groups, SGU_CHUNK, SGU_CHUNK), BF16),
                        pltpu.VMEM((SGU_CHUNK, d), F32)],
        compiler_params=_params(1),
        name="sgu_out",
    )(zz, zz, ln_g, ln_b, w_s, bcol, w_out, x, gpost, gnext)


TM_CONV = 1024
SUB_CONV = 256
TM_FFN = 2176
SUB_FFN = 272
SUB_SGU = 544
TM_TAIL = 512
SUB_TAIL = 128
TM_DOWN = 512
SUB_DOWN = 128
TN_CONV = 512
TN_FFN = 512
TN_SGU = 1024


def kernel(x_prompt, x_sample, cache_conv, norm_mix_pre, norm_mix_post, norm_ffn_pre, norm_ffn_post, a_w_in, a_conv_w, a_w_out, b_w_in, b_b_in, b_ln_g, b_ln_b, b_w_s, b_b_s, b_w_out, ffn_w_gate, ffn_w_up, ffn_w_down):
    batch, seq, d = x_prompt.shape
    dec_batch, dec_seq, _ = x_sample.shape
    depth = norm_mix_pre.shape[0]
    mp, ms = batch * seq, dec_batch * dec_seq
    assert depth == 2 and seq % TM_CONV == 0 and ms % SUB_CONV == 0
    assert SGU_CHUNK % dec_seq == 0 and SUB_CONV % dec_seq == 0 and dec_seq >= CONV_W - 1
    assert mp % TM_TAIL == 0 and ms % TM_TAIL == 0
    assert seq % SGU_CHUNK == 0 and TM_DOWN % SGU_CHUNK == 0 and ms % TM_DOWN == 0
    assert (mp + ms) % TM_FFN == 0

    xp = x_prompt.reshape(mp, d)
    xs = x_sample.reshape(ms, d)
    row = lambda p, i: p[i].reshape(1, -1)

    hist = jnp.pad(cache_conv[0], ((0, 0), (0, dec_seq - (CONV_W - 1)), (0, 0)))
    p_p, p_s, tails, cz_s, w_out_bf = _conv_in(
        xp, xs, row(norm_mix_pre, 0), a_w_in, a_conv_w[0], hist.reshape(ms, d), a_w_out,
        layer=0, seq=seq, dec_seq=dec_seq, tm=TM_CONV, tn=TN_CONV, sub=SUB_CONV)
    state_conv_prompt = tails[None, :, SUBLANES - (CONV_W - 1):, :]
    state_conv_sample = cz_s.reshape(dec_batch, dec_seq, d)[None, :, dec_seq - (CONV_W - 1):, :]
    x, h = _proj_out((p_p, p_s), w_out_bf, (xp, xs), row(norm_mix_post, 0),
                     row(norm_ffn_pre, 0), mp=mp, tm=TM_TAIL, sub=SUB_TAIL)

    a, w_down_bf = _ffn_up(h, ffn_w_gate, ffn_w_up, ffn_w_down, layer=0,
                           tm=TM_FFN, tn=TN_FFN, sub=SUB_FFN)
    x, h = _proj_out(a, w_down_bf, x, row(norm_ffn_post, 0),
                     row(norm_mix_pre, 1), mp=mp, tm=TM_DOWN, sub=SUB_DOWN)

    zz, w_out_bf = _sgu_in(h, b_w_in, b_b_in[:, None, :], b_w_out, layer=0,
                           tm=TM_FFN, tn=TN_SGU, sub=SUB_SGU)
    reps = SGU_CHUNK // dec_seq
    b_s = b_b_s[0]
    b_rows = jnp.stack([b_s, jnp.tile(b_s[:, :dec_seq], (1, reps))])
    bcol = jnp.swapaxes(b_rows, 1, 2)
    x, h, vn_s = _sgu_out(zz, b_ln_g[0].reshape(1, -1), b_ln_b[0].reshape(1, -1),
                          b_w_s, bcol, w_out_bf, x,
                          row(norm_mix_post, 1), row(norm_ffn_pre, 1),
                          layer=0, mp=mp, dec_seq=dec_seq, tm=TM_DOWN)
    state_sgu_v_sample = vn_s.reshape(1, dec_batch, dec_seq, d)

    a, w_down_bf = _ffn_up(h, ffn_w_gate, ffn_w_up, ffn_w_down, layer=1,
                           tm=TM_FFN, tn=TN_FFN, sub=SUB_FFN)
    yp, ys = _proj_out(a, w_down_bf, x, row(norm_ffn_post, 1),
                       None, mp=mp, tm=TM_DOWN, sub=SUB_DOWN)

    return (yp.reshape(batch, seq, d), ys.reshape(dec_batch, dec_seq, d),
            state_conv_prompt, state_conv_sample, state_sgu_v_sample)
```

```python
import functools
import math

import jax
import jax.numpy as jnp
from jax import lax
from jax.experimental import pallas as pl
from jax.experimental.pallas import tpu as pltpu

EPS = 1e-6
CONV_W = 3
SGU_CHUNK = 128
SGU_GROUP_DIM = 128
SUBLANES = 8
BF16_ROWS = 16

MIB = 1024 * 1024
VMEM_LIMIT_BYTES = 60 * MIB

F32 = jnp.float32
BF16 = jnp.bfloat16


def _params(n_axes):
    return pltpu.CompilerParams(
        dimension_semantics=("arbitrary",) * n_axes,
        vmem_limit_bytes=VMEM_LIMIT_BYTES)


def _rms(x, g):
    y = x * lax.rsqrt(jnp.mean(x * x, axis=-1, keepdims=True) + EPS)
    return y * g


def _dot(a, b):
    return jnp.dot(a, b, preferred_element_type=F32)


def _resident(block_shape, index_map):
    return pl.BlockSpec(block_shape, index_map, pipeline_mode=pl.Buffered(1))


def _round_specs(w, layer, n_inner, n_steps):
    k, d = w.shape[1:]
    assert k % n_steps == 0 and (k // n_steps) % BF16_ROWS == 0
    slab = k // n_steps
    step = lambda n, m: n * n_inner + m
    return (pl.BlockSpec((None, slab, d), lambda n, m: (layer, step(n, m), 0)),
            pl.BlockSpec((slab, d), lambda n, m: (step(n, m), 0)),
            jax.ShapeDtypeStruct((k, d), BF16))


def _conv_in_kernel(*refs, blocks_per_seq, dec_seq, sub, sample):
    m = pl.program_id(1)
    if sample:
        (x_ref, g_ref, wbf_ref, cw_ref, hist_ref, p_ref, state_ref) = refs
    else:
        (x_ref, g_ref, wb_ref, wc_ref, wz_ref, cw_ref, wnext_ref,
         p_ref, state_ref, wnext_bf_ref, wbf_ref, carry_ref) = refs
        wnext_bf_ref[...] = wnext_ref[...].astype(BF16)

        @pl.when(m == 0)
        def _():
            wbf_ref[0] = wb_ref[...].astype(BF16)
            wbf_ref[1] = wc_ref[...].astype(BF16)
            wbf_ref[2] = wz_ref[...].astype(BF16)

    tm, tn = p_ref.shape
    if not sample:
        @pl.when(m % blocks_per_seq == 0)
        def _():
            carry_ref[...] = jnp.zeros_like(carry_ref)

        prev = carry_ref[...]

    g = g_ref[...]
    cw = cw_ref[...]
    row = lax.broadcasted_iota(jnp.int32, (sub, tn), 0)
    for s in range(tm // sub):
        rows = pl.ds(s * sub, sub)
        h = _rms(x_ref[rows, :], g).astype(BF16)
        gate_b = _dot(h, wbf_ref[0])
        cz = _dot(h, wbf_ref[1]) * _dot(h, wbf_ref[2])

        if sample:
            hist = hist_ref[rows, :]
            pos = row % dec_seq
            cz1 = jnp.where(pos == 0, pltpu.roll(hist, sub - 1, 0), pltpu.roll(cz, 1, 0))
            cz2 = jnp.where(pos < CONV_W - 1, hist, pltpu.roll(cz, 2, 0))
            state_ref[rows, :] = cz
        else:
            prev1 = prev[SUBLANES - 1:SUBLANES, :]
            prev2 = prev[SUBLANES - 2:SUBLANES - 1, :]
            cz1 = jnp.where(row == 0, prev1, pltpu.roll(cz, 1, 0))
            cz2 = jnp.where(row == 0, prev2,
                            jnp.where(row == 1, prev1, pltpu.roll(cz, 2, 0)))
            prev = cz[sub - SUBLANES:, :]

        conv = cw[0:1, :] * cz2 + cw[1:2, :] * cz1 + cw[2:3, :] * cz
        p_ref[rows, :] = (gate_b * conv).astype(p_ref.dtype)

    if not sample:
        carry_ref[...] = prev
        state_ref[0] = prev


def _conv_in(x, g, w_in, conv_w, hist, w_next, *, layer, seq, dec_seq, tm, tn, sub):
    sample = hist is not None
    rows, d = x.shape
    nt = d // tn
    bps = seq // tm
    kernel = functools.partial(_conv_in_kernel, blocks_per_seq=bps,
                               dec_seq=dec_seq, sub=sub, sample=sample)
    row_tile = pl.BlockSpec((tm, tn), lambda n, m: (m, n))
    x_spec = pl.BlockSpec((tm, d), lambda n, m: (m, 0))
    g_spec = pl.BlockSpec((1, d), lambda n, m: (0, 0))
    cw_spec = pl.BlockSpec((CONV_W, tn), lambda n, m: (0, n))
    wbf_spec = pl.BlockSpec((3, d, tn), lambda n, m: (0, 0, n))
    p_shape = jax.ShapeDtypeStruct((rows, d), BF16)
    if sample:
        return pl.pallas_call(
            kernel,
            grid=(nt, rows // tm),
            in_specs=[x_spec, g_spec, wbf_spec, cw_spec, row_tile],
            out_specs=[row_tile, row_tile],
            out_shape=[p_shape, jax.ShapeDtypeStruct((rows, d), F32)],
            compiler_params=_params(2),
            name="conv_in_sample",
        )(x, g, w_in, conv_w, hist)
    w_in_spec, w_out_spec, w_shape = _round_specs(w_next, layer, rows // tm,
                                                  nt * (rows // tm))
    return pl.pallas_call(
        kernel,
        grid=(nt, rows // tm),
        in_specs=[
            x_spec, g_spec,
            pl.BlockSpec((None, d, tn), lambda n, m: (layer, 0, n)),
            pl.BlockSpec((None, d, tn), lambda n, m: (layer, 0, nt + n)),
            pl.BlockSpec((None, d, tn), lambda n, m: (layer, 0, 2 * nt + n)),
            cw_spec, w_in_spec,
        ],
        out_specs=[
            row_tile,
            pl.BlockSpec((1, SUBLANES, tn), lambda n, m: (m // bps, 0, n)),
            w_out_spec,
            wbf_spec,
        ],
        out_shape=[p_shape, jax.ShapeDtypeStruct((rows // seq, SUBLANES, d), F32),
                   w_shape, jax.ShapeDtypeStruct((3, d, d), BF16)],
        scratch_shapes=[pltpu.VMEM((SUBLANES, tn), F32)],
        compiler_params=_params(2),
        name="conv_in_prompt",
    )(x, g, w_in, w_in, w_in, conv_w, w_next)


def _proj_out_kernel(*refs, n_prompt_blocks, split_in, final, sub):
    m = pl.program_id(0)
    refs = list(refs)
    if split_in:
        ap_ref, as_ref, wbf_ref, xp_ref, xs_ref = refs[:5]
        refs = refs[5:]
    else:
        a_ref, wbf_ref, x_ref = refs[:3]
        refs = refs[3:]
    gpost_ref = refs[0]
    refs = refs[1:]
    if not final:
        gnext_ref = refs[0]
        refs = refs[1:]
    o0_ref, o1_ref = refs

    is_prompt = m < n_prompt_blocks

    def run(a_ref, x_ref, xo_ref, ho_ref):
        for s in range(a_ref.shape[0] // sub):
            rows = pl.ds(s * sub, sub)
            y = _dot(a_ref[rows, :], wbf_ref[...])
            xn = x_ref[rows, :] + _rms(y, gpost_ref[...])
            xo_ref[rows, :] = xn
            if ho_ref is not None:
                ho_ref[rows, :] = _rms(xn, gnext_ref[...]).astype(ho_ref.dtype)

    if split_in:
        pl.when(is_prompt)(lambda: run(ap_ref, xp_ref, o0_ref, o1_ref))
        pl.when(jnp.logical_not(is_prompt))(lambda: run(as_ref, xs_ref, o0_ref, o1_ref))
    elif final:
        pl.when(is_prompt)(lambda: run(a_ref, x_ref, o0_ref, None))
        pl.when(jnp.logical_not(is_prompt))(lambda: run(a_ref, x_ref, o1_ref, None))
    else:
        run(a_ref, x_ref, o0_ref, o1_ref)


def _proj_out(a, w, res, gpost, gnext, *, mp, tm, sub):
    final = gnext is None
    split_in = isinstance(a, tuple)
    assert split_in == isinstance(res, tuple)
    k, d = w.shape
    m_all = sum(t.shape[0] for t in a) if split_in else a.shape[0]
    ms = m_all - mp
    npb, nsb = mp // tm, ms // tm
    prompt_block = lambda m: (jnp.minimum(m, npb - 1), 0)
    sample_block = lambda m: (jnp.maximum(m - npb, 0), 0)
    row_block = lambda m: (m, 0)
    w_spec = _resident((k, d), lambda m: (0, 0))
    if split_in:
        in_specs = [pl.BlockSpec((tm, k), prompt_block), pl.BlockSpec((tm, k), sample_block),
                    w_spec,
                    pl.BlockSpec((tm, d), prompt_block), pl.BlockSpec((tm, d), sample_block)]
        args = (*a, w, *res)
    else:
        in_specs = [pl.BlockSpec((tm, k), row_block), w_spec,
                    pl.BlockSpec((tm, d), row_block)]
        args = (a, w, res)
    norm_args = (gpost,) if final else (gpost, gnext)
    in_specs += [_resident((1, d), lambda m: (0, 0)) for _ in norm_args]
    if final:
        out_specs = [pl.BlockSpec((tm, d), prompt_block),
                     pl.BlockSpec((tm, d), sample_block)]
        out_shape = [jax.ShapeDtypeStruct((mp, d), F32),
                     jax.ShapeDtypeStruct((ms, d), F32)]
    else:
        out_specs = [pl.BlockSpec((tm, d), row_block),
                     pl.BlockSpec((tm, d), row_block)]
        out_shape = [jax.ShapeDtypeStruct((m_all, d), F32),
                     jax.ShapeDtypeStruct((m_all, d), BF16)]
    kernel = functools.partial(_proj_out_kernel, n_prompt_blocks=npb,
                               split_in=split_in, final=final, sub=sub)
    return pl.pallas_call(
        kernel,
        grid=(npb + nsb,),
        in_specs=in_specs,
        out_specs=out_specs,
        out_shape=out_shape,
        compiler_params=_params(1),
        name="proj_out_k%d" % k,
    )(*args, *norm_args)


def _ffn_up_kernel(h_ref, wg_ref, wu_ref, wnext_ref, a_ref, wnext_bf_ref, wbf_ref, *, sub):
    wnext_bf_ref[...] = wnext_ref[...].astype(BF16)

    @pl.when(pl.program_id(1) == 0)
    def _():
        wbf_ref[0] = wg_ref[...].astype(BF16)
        wbf_ref[1] = wu_ref[...].astype(BF16)

    for s in range(a_ref.shape[0] // sub):
        rows = pl.ds(s * sub, sub)
        h = h_ref[rows, :]
        g = _dot(h, wbf_ref[0])
        u = _dot(h, wbf_ref[1])
        a_ref[rows, :] = (g * jax.nn.sigmoid(g) * u).astype(a_ref.dtype)


def _ffn_up(h, w_gate, w_up, w_next, *, layer, tm, tn, sub):
    m_all, d = h.shape
    f = w_gate.shape[2]
    grid = (f // tn, m_all // tm)
    w_in_spec, w_out_spec, w_shape = _round_specs(w_next, layer, grid[1], grid[0] * grid[1])
    return pl.pallas_call(
        functools.partial(_ffn_up_kernel, sub=sub),
        grid=grid,
        in_specs=[
            pl.BlockSpec((tm, d), lambda n, m: (m, 0)),
            pl.BlockSpec((None, d, tn), lambda n, m: (layer, 0, n)),
            pl.BlockSpec((None, d, tn), lambda n, m: (layer, 0, n)),
            w_in_spec,
        ],
        out_specs=[pl.BlockSpec((tm, tn), lambda n, m: (m, n)), w_out_spec],
        out_shape=[jax.ShapeDtypeStruct((m_all, f), BF16), w_shape],
        scratch_shapes=[pltpu.VMEM((2, d, tn), BF16)],
        compiler_params=_params(2),
        name="ffn_up",
    )(h, w_gate, w_up, w_next)


def _sgu_in_kernel(h_ref, w_ref, b_ref, wnext_ref, zz_ref, wnext_bf_ref, wbf_ref, *, sub):
    wnext_bf_ref[...] = wnext_ref[...].astype(BF16)

    @pl.when(pl.program_id(1) == 0)
    def _():
        wbf_ref[...] = w_ref[...].astype(BF16)

    for s in range(zz_ref.shape[0] // sub):
        rows = pl.ds(s * sub, sub)
        z = _dot(h_ref[rows, :], wbf_ref[...]) + b_ref[...]
        gelu = 0.5 * z * (1.0 + lax.erf(z * math.sqrt(0.5)))
        zz_ref[rows, :] = gelu.astype(zz_ref.dtype)


def _sgu_in(h, w, b, w_next, *, layer, tm, tn, sub):
    m_all, d = h.shape
    n_out = w.shape[2]
    grid = (n_out // tn, m_all // tm)
    w_in_spec, w_out_spec, w_shape = _round_specs(w_next, layer, grid[1], grid[0] * grid[1])
    return pl.pallas_call(
        functools.partial(_sgu_in_kernel, sub=sub),
        grid=grid,
        in_specs=[
            pl.BlockSpec((tm, d), lambda n, m: (m, 0)),
            pl.BlockSpec((None, d, tn), lambda n, m: (layer, 0, n)),
            pl.BlockSpec((None, 1, tn), lambda n, m: (layer, 0, n)),
            w_in_spec,
        ],
        out_specs=[pl.BlockSpec((tm, tn), lambda n, m: (m, n)), w_out_spec],
        out_shape=[jax.ShapeDtypeStruct((m_all, n_out), BF16), w_shape],
        scratch_shapes=[pltpu.VMEM((d, tn), BF16)],
        compiler_params=_params(2),
        name="sgu_in",
    )(h, w, b, w_next)


def _sgu_out_kernel(u_ref, v_ref, lng_ref, lnb_ref, ws_ref, bcol_ref, wbf_ref,
                    x_ref, gpost_ref, gnext_ref,
                    xo_ref, ho_ref, vn_ref,
                    mixw_ref, mixb_ref,
                    *, n_prompt_blocks, dec_seq):
    m = pl.program_id(0)
    tm, d = x_ref.shape
    n_groups = d // SGU_GROUP_DIM

    def build_mixing(sample_kind):
        i = lax.broadcasted_iota(jnp.int32, (SGU_CHUNK, SGU_CHUNK), 0)
        j = lax.broadcasted_iota(jnp.int32, (SGU_CHUNK, SGU_CHUNK), 1)
        mask = i >= j
        if sample_kind:
            mask = mask & ((i // dec_seq) == (j // dec_seq))
        reps = SGU_CHUNK // dec_seq
        bcol = bcol_ref[int(sample_kind)]

        def masked(g):
            if sample_kind:
                w = ws_ref[g, :dec_seq, :dec_seq]
                w = jnp.concatenate([w] * reps, axis=0)
                w = jnp.concatenate([w] * reps, axis=1)
            else:
                w = ws_ref[g]
            return jnp.where(mask, w, 0.0).astype(BF16)

        for g in range(n_groups):
            cols = slice(g * SGU_GROUP_DIM, (g + 1) * SGU_GROUP_DIM)
            mixb_ref[:, cols] = jnp.broadcast_to(bcol[:, g:g + 1],
                                                 (SGU_CHUNK, SGU_GROUP_DIM))
        for q in range(n_groups // 2):
            mixw_ref[q] = jnp.concatenate([masked(2 * q), masked(2 * q + 1)], axis=1)

    pl.when(m == 0)(lambda: build_mixing(False))
    pl.when(m == n_prompt_blocks)(lambda: build_mixing(True))

    for c in range(tm // SGU_CHUNK):
        rows = pl.ds(c * SGU_CHUNK, SGU_CHUNK)
        v = v_ref[rows, :].astype(F32)
        mu = jnp.mean(v, axis=-1, keepdims=True)
        vc = v - mu
        vn = vc * lax.rsqrt(jnp.mean(vc * vc, axis=-1, keepdims=True) + EPS)
        vn = vn * lng_ref[...] + lnb_ref[...]
        vn_ref[rows, :] = vn
        vnb = vn.astype(BF16)

        gated = []
        zero = jnp.zeros((SGU_CHUNK, SGU_GROUP_DIM), BF16)
        for q in range(n_groups // 2):
            lo = 2 * q * SGU_GROUP_DIM
            mid, hi = lo + SGU_GROUP_DIM, lo + 2 * SGU_GROUP_DIM
            v_pair = jnp.concatenate(
                [jnp.concatenate([vnb[:, lo:mid], zero], axis=1),
                 jnp.concatenate([zero, vnb[:, mid:hi]], axis=1)], axis=0)
            mixed = _dot(mixw_ref[q], v_pair) + mixb_ref[:, lo:hi]
            gated.append((u_ref[rows, lo:hi].astype(F32) * mixed).astype(BF16))
        y = _dot(jnp.concatenate(gated, axis=1), wbf_ref[...])

        xn = x_ref[rows, :] + _rms(y, gpost_ref[...])
        xo_ref[rows, :] = xn
        ho_ref[rows, :] = _rms(xn, gnext_ref[...]).astype(ho_ref.dtype)


def _sgu_out(zz, ln_g, ln_b, w_s, bcol, w_out, x, gpost, gnext,
             *, layer, mp, dec_seq, tm):
    m_all, d = x.shape
    ms = m_all - mp
    npb, nsb = mp // tm, ms // tm
    n_groups = d // SGU_GROUP_DIM
    row_block = lambda m: (m, 0)
    const2 = lambda m: (0, 0)
    kernel = functools.partial(_sgu_out_kernel, n_prompt_blocks=npb, dec_seq=dec_seq)
    return pl.pallas_call(
        kernel,
        grid=(npb + nsb,),
        in_specs=[
            pl.BlockSpec((tm, d), lambda m: (m, 0)),
            pl.BlockSpec((tm, d), lambda m: (m, 1)),
            _resident((1, d), const2),
            _resident((1, d), const2),
            _resident((None, n_groups, SGU_CHUNK, SGU_CHUNK),
                      lambda m: (layer, 0, 0, 0)),
            _resident((2, SGU_CHUNK, n_groups), lambda m: (0, 0, 0)),
            _resident((d, d), const2),
            pl.BlockSpec((tm, d), row_block),
            _resident((1, d), const2),
            _resident((1, d), const2),
        ],
        out_specs=[
            pl.BlockSpec((tm, d), row_block),
            pl.BlockSpec((tm, d), row_block),
            pl.BlockSpec((tm, d), lambda m: (jnp.maximum(m - npb, 0), 0)),
        ],
        out_shape=[
            jax.ShapeDtypeStruct((m_all, d), F32),
            jax.ShapeDtypeStruct((m_all, d), BF16),
            jax.ShapeDtypeStruct((ms, d), F32),
        ],
        scratch_shapes=[pltpu.VMEM((n_groups // 2, SGU_CHUNK, 2 * SGU_CHUNK), BF16),
                        pltpu.VMEM((SGU_CHUNK, d), F32)],
        compiler_params=_params(1),
        name="sgu_out",
    )(zz, zz, ln_g, ln_b, w_s, bcol, w_out, x, gpost, gnext)


TM_CONV = 1024
TM_SAMPLE = 512
SUB_CONV = 256
TM_FFN = 2176
SUB_FFN = 272
SUB_SGU = 544
TM_TAIL = 512
SUB_TAIL = 128
TM_DOWN = 512
SUB_DOWN = 128
TN_CONV = 512
TN_FFN = 512
TN_SGU = 1024


def kernel(x_prompt, x_sample, cache_conv, norm_mix_pre, norm_mix_post, norm_ffn_pre, norm_ffn_post, a_w_in, a_conv_w, a_w_out, b_w_in, b_b_in, b_ln_g, b_ln_b, b_w_s, b_b_s, b_w_out, ffn_w_gate, ffn_w_up, ffn_w_down):
    batch, seq, d = x_prompt.shape
    dec_batch, dec_seq, _ = x_sample.shape
    depth = norm_mix_pre.shape[0]
    mp, ms = batch * seq, dec_batch * dec_seq
    assert depth == 2 and seq % TM_CONV == 0 and ms % TM_SAMPLE == 0
    assert SGU_CHUNK % dec_seq == 0 and SUB_CONV % dec_seq == 0 and dec_seq >= CONV_W - 1
    assert mp % TM_TAIL == 0 and ms % TM_TAIL == 0
    assert seq % SGU_CHUNK == 0 and TM_DOWN % SGU_CHUNK == 0 and ms % TM_DOWN == 0
    assert (mp + ms) % TM_FFN == 0

    xp = x_prompt.reshape(mp, d)
    xs = x_sample.reshape(ms, d)
    row = lambda p, i: p[i].reshape(1, -1)

    conv_args = dict(layer=0, seq=seq, dec_seq=dec_seq, tn=TN_CONV, sub=SUB_CONV)
    p_p, tails, w_out_bf, w_in_bf = _conv_in(xp, row(norm_mix_pre, 0), a_w_in, a_conv_w[0],
                                             None, a_w_out, tm=TM_CONV, **conv_args)
    hist = jnp.pad(cache_conv[0], ((0, 0), (0, dec_seq - (CONV_W - 1)), (0, 0)))
    p_s, cz_s = _conv_in(xs, row(norm_mix_pre, 0), w_in_bf, a_conv_w[0],
                         hist.reshape(ms, d), None, tm=TM_SAMPLE, **conv_args)
    state_conv_prompt = tails[None, :, SUBLANES - (CONV_W - 1):, :]
    state_conv_sample = cz_s.reshape(dec_batch, dec_seq, d)[None, :, dec_seq - (CONV_W - 1):, :]
    x, h = _proj_out((p_p, p_s), w_out_bf, (xp, xs), row(norm_mix_post, 0),
                     row(norm_ffn_pre, 0), mp=mp, tm=TM_TAIL, sub=SUB_TAIL)

    a, w_down_bf = _ffn_up(h, ffn_w_gate, ffn_w_up, ffn_w_down, layer=0,
                           tm=TM_FFN, tn=TN_FFN, sub=SUB_FFN)
    x, h = _proj_out(a, w_down_bf, x, row(norm_ffn_post, 0),
                     row(norm_mix_pre, 1), mp=mp, tm=TM_DOWN, sub=SUB_DOWN)

    zz, w_out_bf = _sgu_in(h, b_w_in, b_b_in[:, None, :], b_w_out, layer=0,
                           tm=TM_FFN, tn=TN_SGU, sub=SUB_SGU)
    reps = SGU_CHUNK // dec_seq
    b_s = b_b_s[0]
    b_rows = jnp.stack([b_s, jnp.tile(b_s[:, :dec_seq], (1, reps))])
    bcol = jnp.swapaxes(b_rows, 1, 2)
    x, h, vn_s = _sgu_out(zz, b_ln_g[0].reshape(1, -1), b_ln_b[0].reshape(1, -1),
                          b_w_s, bcol, w_out_bf, x,
                          row(norm_mix_post, 1), row(norm_ffn_pre, 1),
                          layer=0, mp=mp, dec_seq=dec_seq, tm=TM_DOWN)
    state_sgu_v_sample = vn_s.reshape(1, dec_batch, dec_seq, d)

    a, w_down_bf = _ffn_up(h, ffn_w_gate, ffn_w_up, ffn_w_down, layer=1,
                           tm=TM_FFN, tn=TN_FFN, sub=SUB_FFN)
    yp, ys = _proj_out(a, w_down_bf, x, row(norm_ffn_post, 1),
                       None, mp=mp, tm=TM_DOWN, sub=SUB_DOWN)

    return (yp.reshape(batch, seq, d), ys.reshape(dec_batch, dec_seq, d),
            state_conv_prompt, state_conv_sample, state_sgu_v_sample)
```

```python
import functools
import math

import jax
import jax.numpy as jnp
from jax import lax
from jax.experimental import pallas as pl
from jax.experimental.pallas import tpu as pltpu

EPS = 1e-6
CONV_W = 3
SGU_CHUNK = 128
SGU_GROUP_DIM = 128
SUBLANES = 8
BF16_ROWS = 16

MIB = 1024 * 1024
VMEM_LIMIT_BYTES = 60 * MIB

F32 = jnp.float32
BF16 = jnp.bfloat16


def _params(n_axes):
    return pltpu.CompilerParams(
        dimension_semantics=("arbitrary",) * n_axes,
        vmem_limit_bytes=VMEM_LIMIT_BYTES)


def _rms(x, g):
    y = x * lax.rsqrt(jnp.mean(x * x, axis=-1, keepdims=True) + EPS)
    return y * g


def _dot(a, b):
    return jnp.dot(a, b, preferred_element_type=F32)


def _resident(block_shape, index_map):
    return pl.BlockSpec(block_shape, index_map, pipeline_mode=pl.Buffered(1))


def _round_specs(w, layer, n_inner, n_steps):
    k, d = w.shape[1:]
    assert k % n_steps == 0 and (k // n_steps) % BF16_ROWS == 0
    slab = k // n_steps
    step = lambda n, m: n * n_inner + m
    return (pl.BlockSpec((None, slab, d), lambda n, m: (layer, step(n, m), 0)),
            pl.BlockSpec((slab, d), lambda n, m: (step(n, m), 0)),
            jax.ShapeDtypeStruct((k, d), BF16))


def _conv_in_kernel(*refs, blocks_per_seq, dec_seq, sub, sample):
    if sample:
        (x_ref, g_ref, wb_ref, wc_ref, wz_ref, cw_ref, hist_ref,
         p_ref, state_ref, wbf_ref) = refs
    else:
        (x_ref, g_ref, wb_ref, wc_ref, wz_ref, cw_ref, wnext_ref,
         p_ref, state_ref, wnext_bf_ref, wbf_ref, carry_ref) = refs
        wnext_bf_ref[...] = wnext_ref[...].astype(BF16)
    m = pl.program_id(1)
    tm, tn = p_ref.shape

    @pl.when(m == 0)
    def _():
        wbf_ref[0] = wb_ref[...].astype(BF16)
        wbf_ref[1] = wc_ref[...].astype(BF16)
        wbf_ref[2] = wz_ref[...].astype(BF16)

    if not sample:
        @pl.when(m % blocks_per_seq == 0)
        def _():
            carry_ref[...] = jnp.zeros_like(carry_ref)

        prev = carry_ref[...]

    g = g_ref[...]
    cw = cw_ref[...]
    row = lax.broadcasted_iota(jnp.int32, (sub, tn), 0)
    for s in range(tm // sub):
        rows = pl.ds(s * sub, sub)
        h = _rms(x_ref[rows, :], g).astype(BF16)
        gate_b = _dot(h, wbf_ref[0])
        cz = _dot(h, wbf_ref[1]) * _dot(h, wbf_ref[2])

        if sample:
            hist = hist_ref[rows, :]
            pos = row % dec_seq
            cz1 = jnp.where(pos == 0, pltpu.roll(hist, sub - 1, 0), pltpu.roll(cz, 1, 0))
            cz2 = jnp.where(pos < CONV_W - 1, hist, pltpu.roll(cz, 2, 0))
            state_ref[rows, :] = cz
        else:
            prev1 = prev[SUBLANES - 1:SUBLANES, :]
            prev2 = prev[SUBLANES - 2:SUBLANES - 1, :]
            cz1 = jnp.where(row == 0, prev1, pltpu.roll(cz, 1, 0))
            cz2 = jnp.where(row == 0, prev2,
                            jnp.where(row == 1, prev1, pltpu.roll(cz, 2, 0)))
            prev = cz[sub - SUBLANES:, :]

        conv = cw[0:1, :] * cz2 + cw[1:2, :] * cz1 + cw[2:3, :] * cz
        p_ref[rows, :] = (gate_b * conv).astype(p_ref.dtype)

    if not sample:
        carry_ref[...] = prev
        state_ref[0] = prev


def _conv_in(x, g, w_in, conv_w, hist, w_next, *, layer, seq, dec_seq, tm, tn, sub):
    sample = hist is not None
    rows, d = x.shape
    nt = d // tn
    bps = seq // tm
    kernel = functools.partial(_conv_in_kernel, blocks_per_seq=bps,
                               dec_seq=dec_seq, sub=sub, sample=sample)
    row_tile = pl.BlockSpec((tm, tn), lambda n, m: (m, n))
    in_specs = [
        pl.BlockSpec((tm, d), lambda n, m: (m, 0)),
        pl.BlockSpec((1, d), lambda n, m: (0, 0)),
        pl.BlockSpec((None, d, tn), lambda n, m: (layer, 0, n)),
        pl.BlockSpec((None, d, tn), lambda n, m: (layer, 0, nt + n)),
        pl.BlockSpec((None, d, tn), lambda n, m: (layer, 0, 2 * nt + n)),
        pl.BlockSpec((CONV_W, tn), lambda n, m: (0, n)),
    ]
    scratch = [pltpu.VMEM((3, d, tn), BF16)]
    if sample:
        in_specs.append(row_tile)
        args = (x, g, w_in, w_in, w_in, conv_w, hist)
        state_spec = row_tile
        state_shape = jax.ShapeDtypeStruct((rows, d), F32)
    else:
        args = (x, g, w_in, w_in, w_in, conv_w, w_next)
        state_spec = pl.BlockSpec((1, SUBLANES, tn), lambda n, m: (m // bps, 0, n))
        state_shape = jax.ShapeDtypeStruct((rows // seq, SUBLANES, d), F32)
        scratch.append(pltpu.VMEM((SUBLANES, tn), F32))
        w_in_spec, w_out_spec, w_shape = _round_specs(w_next, layer, rows // tm,
                                                      nt * (rows // tm))
        in_specs.append(w_in_spec)
    out_specs = [row_tile, state_spec] + ([] if sample else [w_out_spec])
    out_shape = [jax.ShapeDtypeStruct((rows, d), BF16), state_shape]
    out_shape += [] if sample else [w_shape]
    return pl.pallas_call(
        kernel,
        grid=(nt, rows // tm),
        in_specs=in_specs,
        out_specs=out_specs,
        out_shape=out_shape,
        scratch_shapes=scratch,
        compiler_params=_params(2),
        name="conv_in_sample" if sample else "conv_in_prompt",
    )(*args)


def _proj_out_kernel(*refs, n_prompt_blocks, split_in, final, sub):
    m = pl.program_id(0)
    refs = list(refs)
    if split_in:
        ap_ref, as_ref, wbf_ref, xp_ref, xs_ref = refs[:5]
        refs = refs[5:]
    else:
        a_ref, wbf_ref, x_ref = refs[:3]
        refs = refs[3:]
    gpost_ref = refs[0]
    refs = refs[1:]
    if not final:
        gnext_ref = refs[0]
        refs = refs[1:]
    o0_ref, o1_ref = refs

    is_prompt = m < n_prompt_blocks

    def run(a_ref, x_ref, xo_ref, ho_ref):
        for s in range(a_ref.shape[0] // sub):
            rows = pl.ds(s * sub, sub)
            y = _dot(a_ref[rows, :], wbf_ref[...])
            xn = x_ref[rows, :] + _rms(y, gpost_ref[...])
            xo_ref[rows, :] = xn
            if ho_ref is not None:
                ho_ref[rows, :] = _rms(xn, gnext_ref[...]).astype(ho_ref.dtype)

    if split_in:
        pl.when(is_prompt)(lambda: run(ap_ref, xp_ref, o0_ref, o1_ref))
        pl.when(jnp.logical_not(is_prompt))(lambda: run(as_ref, xs_ref, o0_ref, o1_ref))
    elif final:
        pl.when(is_prompt)(lambda: run(a_ref, x_ref, o0_ref, None))
        pl.when(jnp.logical_not(is_prompt))(lambda: run(a_ref, x_ref, o1_ref, None))
    else:
        run(a_ref, x_ref, o0_ref, o1_ref)


def _proj_out(a, w, res, gpost, gnext, *, mp, tm, sub):
    final = gnext is None
    split_in = isinstance(a, tuple)
    assert split_in == isinstance(res, tuple)
    k, d = w.shape
    m_all = sum(t.shape[0] for t in a) if split_in else a.shape[0]
    ms = m_all - mp
    npb, nsb = mp // tm, ms // tm
    prompt_block = lambda m: (jnp.minimum(m, npb - 1), 0)
    sample_block = lambda m: (jnp.maximum(m - npb, 0), 0)
    row_block = lambda m: (m, 0)
    w_spec = _resident((k, d), lambda m: (0, 0))
    if split_in:
        in_specs = [pl.BlockSpec((tm, k), prompt_block), pl.BlockSpec((tm, k), sample_block),
                    w_spec,
                    pl.BlockSpec((tm, d), prompt_block), pl.BlockSpec((tm, d), sample_block)]
        args = (*a, w, *res)
    else:
        in_specs = [pl.BlockSpec((tm, k), row_block), w_spec,
                    pl.BlockSpec((tm, d), row_block)]
        args = (a, w, res)
    norm_args = (gpost,) if final else (gpost, gnext)
    in_specs += [_resident((1, d), lambda m: (0, 0)) for _ in norm_args]
    if final:
        out_specs = [pl.BlockSpec((tm, d), prompt_block),
                     pl.BlockSpec((tm, d), sample_block)]
        out_shape = [jax.ShapeDtypeStruct((mp, d), F32),
                     jax.ShapeDtypeStruct((ms, d), F32)]
    else:
        out_specs = [pl.BlockSpec((tm, d), row_block),
                     pl.BlockSpec((tm, d), row_block)]
        out_shape = [jax.ShapeDtypeStruct((m_all, d), F32),
                     jax.ShapeDtypeStruct((m_all, d), BF16)]
    kernel = functools.partial(_proj_out_kernel, n_prompt_blocks=npb,
                               split_in=split_in, final=final, sub=sub)
    return pl.pallas_call(
        kernel,
        grid=(npb + nsb,),
        in_specs=in_specs,
        out_specs=out_specs,
        out_shape=out_shape,
        compiler_params=_params(1),
        name="proj_out_k%d" % k,
    )(*args, *norm_args)


def _ffn_up_kernel(h_ref, wg_ref, wu_ref, wnext_ref, a_ref, wnext_bf_ref, wbf_ref, *, sub):
    wnext_bf_ref[...] = wnext_ref[...].astype(BF16)

    @pl.when(pl.program_id(1) == 0)
    def _():
        wbf_ref[0] = wg_ref[...].astype(BF16)
        wbf_ref[1] = wu_ref[...].astype(BF16)

    for s in range(a_ref.shape[0] // sub):
        rows = pl.ds(s * sub, sub)
        h = h_ref[rows, :]
        g = _dot(h, wbf_ref[0])
        u = _dot(h, wbf_ref[1])
        a_ref[rows, :] = (g * jax.nn.sigmoid(g) * u).astype(a_ref.dtype)


def _ffn_up(h, w_gate, w_up, w_next, *, layer, tm, tn, sub):
    m_all, d = h.shape
    f = w_gate.shape[2]
    grid = (f // tn, m_all // tm)
    w_in_spec, w_out_spec, w_shape = _round_specs(w_next, layer, grid[1], grid[0] * grid[1])
    return pl.pallas_call(
        functools.partial(_ffn_up_kernel, sub=sub),
        grid=grid,
        in_specs=[
            pl.BlockSpec((tm, d), lambda n, m: (m, 0)),
            pl.BlockSpec((None, d, tn), lambda n, m: (layer, 0, n)),
            pl.BlockSpec((None, d, tn), lambda n, m: (layer, 0, n)),
            w_in_spec,
        ],
        out_specs=[pl.BlockSpec((tm, tn), lambda n, m: (m, n)), w_out_spec],
        out_shape=[jax.ShapeDtypeStruct((m_all, f), BF16), w_shape],
        scratch_shapes=[pltpu.VMEM((2, d, tn), BF16)],
        compiler_params=_params(2),
        name="ffn_up",
    )(h, w_gate, w_up, w_next)


def _sgu_in_kernel(h_ref, w_ref, b_ref, wnext_ref, zz_ref, wnext_bf_ref, wbf_ref, *, sub):
    wnext_bf_ref[...] = wnext_ref[...].astype(BF16)

    @pl.when(pl.program_id(1) == 0)
    def _():
        wbf_ref[...] = w_ref[...].astype(BF16)

    for s in range(zz_ref.shape[0] // sub):
        rows = pl.ds(s * sub, sub)
        z = _dot(h_ref[rows, :], wbf_ref[...]) + b_ref[...]
        gelu = 0.5 * z * (1.0 + lax.erf(z * math.sqrt(0.5)))
        zz_ref[rows, :] = gelu.astype(zz_ref.dtype)


def _sgu_in(h, w, b, w_next, *, layer, tm, tn, sub):
    m_all, d = h.shape
    n_out = w.shape[2]
    grid = (n_out // tn, m_all // tm)
    w_in_spec, w_out_spec, w_shape = _round_specs(w_next, layer, grid[1], grid[0] * grid[1])
    return pl.pallas_call(
        functools.partial(_sgu_in_kernel, sub=sub),
        grid=grid,
        in_specs=[
            pl.BlockSpec((tm, d), lambda n, m: (m, 0)),
            pl.BlockSpec((None, d, tn), lambda n, m: (layer, 0, n)),
            pl.BlockSpec((None, 1, tn), lambda n, m: (layer, 0, n)),
            w_in_spec,
        ],
        out_specs=[pl.BlockSpec((tm, tn), lambda n, m: (m, n)), w_out_spec],
        out_shape=[jax.ShapeDtypeStruct((m_all, n_out), BF16), w_shape],
        scratch_shapes=[pltpu.VMEM((d, tn), BF16)],
        compiler_params=_params(2),
        name="sgu_in",
    )(h, w, b, w_next)


def _sgu_out_kernel(u_ref, v_ref, lng_ref, lnb_ref, ws_ref, bcol_ref, wbf_ref,
                    x_ref, gpost_ref, gnext_ref,
                    xo_ref, ho_ref, vn_ref,
                    mixw_ref, mixb_ref,
                    *, n_prompt_blocks, dec_seq):
    m = pl.program_id(0)
    tm, d = x_ref.shape
    n_groups = d // SGU_GROUP_DIM

    def build_mixing(sample_kind):
        i = lax.broadcasted_iota(jnp.int32, (SGU_CHUNK, SGU_CHUNK), 0)
        j = lax.broadcasted_iota(jnp.int32, (SGU_CHUNK, SGU_CHUNK), 1)
        mask = i >= j
        if sample_kind:
            mask = mask & ((i // dec_seq) == (j // dec_seq))
        reps = SGU_CHUNK // dec_seq
        bcol = bcol_ref[int(sample_kind)]
        for g in range(n_groups):
            cols = slice(g * SGU_GROUP_DIM, (g + 1) * SGU_GROUP_DIM)
            if sample_kind:
                w = ws_ref[g, :dec_seq, :dec_seq]
                w = jnp.concatenate([w] * reps, axis=0)
                w = jnp.concatenate([w] * reps, axis=1)
            else:
                w = ws_ref[g]
            mixw_ref[g] = jnp.where(mask, w, 0.0).astype(BF16)
            mixb_ref[:, cols] = jnp.broadcast_to(bcol[:, g:g + 1],
                                                 (SGU_CHUNK, SGU_GROUP_DIM))

    pl.when(m == 0)(lambda: build_mixing(False))
    pl.when(m == n_prompt_blocks)(lambda: build_mixing(True))

    for c in range(tm // SGU_CHUNK):
        rows = pl.ds(c * SGU_CHUNK, SGU_CHUNK)
        v = v_ref[rows, :].astype(F32)
        mu = jnp.mean(v, axis=-1, keepdims=True)
        vc = v - mu
        vn = vc * lax.rsqrt(jnp.mean(vc * vc, axis=-1, keepdims=True) + EPS)
        vn = vn * lng_ref[...] + lnb_ref[...]
        vn_ref[rows, :] = vn
        vnb = vn.astype(BF16)

        gated = []
        for g in range(n_groups):
            cols = slice(g * SGU_GROUP_DIM, (g + 1) * SGU_GROUP_DIM)
            mixed = _dot(mixw_ref[g], vnb[:, cols]) + mixb_ref[:, cols]
            gated.append((u_ref[rows, cols].astype(F32) * mixed).astype(BF16))
        y = _dot(jnp.concatenate(gated, axis=1), wbf_ref[...])

        xn = x_ref[rows, :] + _rms(y, gpost_ref[...])
        xo_ref[rows, :] = xn
        ho_ref[rows, :] = _rms(xn, gnext_ref[...]).astype(ho_ref.dtype)


def _sgu_out(zz, ln_g, ln_b, w_s, bcol, w_out, x, gpost, gnext,
             *, layer, mp, dec_seq, tm):
    m_all, d = x.shape
    ms = m_all - mp
    npb, nsb = mp // tm, ms // tm
    n_groups = d // SGU_GROUP_DIM
    row_block = lambda m: (m, 0)
    const2 = lambda m: (0, 0)
    kernel = functools.partial(_sgu_out_kernel, n_prompt_blocks=npb, dec_seq=dec_seq)
    return pl.pallas_call(
        kernel,
        grid=(npb + nsb,),
        in_specs=[
            pl.BlockSpec((tm, d), lambda m: (m, 0)),
            pl.BlockSpec((tm, d), lambda m: (m, 1)),
            _resident((1, d), const2),
            _resident((1, d), const2),
            _resident((None, n_groups, SGU_CHUNK, SGU_CHUNK),
                      lambda m: (layer, 0, 0, 0)),
            _resident((2, SGU_CHUNK, n_groups), lambda m: (0, 0, 0)),
            _resident((d, d), const2),
            pl.BlockSpec((tm, d), row_block),
            _resident((1, d), const2),
            _resident((1, d), const2),
        ],
        out_specs=[
            pl.BlockSpec((tm, d), row_block),
            pl.BlockSpec((tm, d), row_block),
            pl.BlockSpec((tm, d), lambda m: (jnp.maximum(m - npb, 0), 0)),
        ],
        out_shape=[
            jax.ShapeDtypeStruct((m_all, d), F32),
            jax.ShapeDtypeStruct((m_all, d), BF16),
            jax.ShapeDtypeStruct((ms, d), F32),
        ],
        scratch_shapes=[pltpu.VMEM((n_groups, SGU_CHUNK, SGU_CHUNK), BF16),
                        pltpu.VMEM((SGU_CHUNK, d), F32)],
        compiler_params=_params(1),
        name="sgu_out",
    )(zz, zz, ln_g, ln_b, w_s, bcol, w_out, x, gpost, gnext)


TM_CONV = 1024
TM_SAMPLE = 512
SUB_CONV = 128
TM_FFN = 2176
SUB_FFN = 272
SUB_SGU = 544
TM_TAIL = 512
SUB_TAIL = 256
TM_DOWN = 512
SUB_DOWN = 256
TN_CONV = 512
TN_FFN = 512
TN_SGU = 1024


def kernel(x_prompt, x_sample, cache_conv, norm_mix_pre, norm_mix_post, norm_ffn_pre, norm_ffn_post, a_w_in, a_conv_w, a_w_out, b_w_in, b_b_in, b_ln_g, b_ln_b, b_w_s, b_b_s, b_w_out, ffn_w_gate, ffn_w_up, ffn_w_down):
    batch, seq, d = x_prompt.shape
    dec_batch, dec_seq, _ = x_sample.shape
    depth = norm_mix_pre.shape[0]
    mp, ms = batch * seq, dec_batch * dec_seq
    assert depth == 2 and seq % TM_CONV == 0 and ms % TM_SAMPLE == 0
    assert SGU_CHUNK % dec_seq == 0 and SUB_CONV % dec_seq == 0 and dec_seq >= CONV_W - 1
    assert mp % TM_TAIL == 0 and ms % TM_TAIL == 0
    assert seq % SGU_CHUNK == 0 and TM_DOWN % SGU_CHUNK == 0 and ms % TM_DOWN == 0
    assert (mp + ms) % TM_FFN == 0

    xp = x_prompt.reshape(mp, d)
    xs = x_sample.reshape(ms, d)
    row = lambda p, i: p[i].reshape(1, -1)

    conv_args = dict(layer=0, seq=seq, dec_seq=dec_seq, tn=TN_CONV, sub=SUB_CONV)
    p_p, tails, w_out_bf = _conv_in(xp, row(norm_mix_pre, 0), a_w_in, a_conv_w[0], None,
                                    a_w_out, tm=TM_CONV, **conv_args)
    hist = jnp.pad(cache_conv[0], ((0, 0), (0, dec_seq - (CONV_W - 1)), (0, 0)))
    p_s, cz_s = _conv_in(xs, row(norm_mix_pre, 0), a_w_in, a_conv_w[0],
                         hist.reshape(ms, d), None, tm=TM_SAMPLE, **conv_args)
    state_conv_prompt = tails[None, :, SUBLANES - (CONV_W - 1):, :]
    state_conv_sample = cz_s.reshape(dec_batch, dec_seq, d)[None, :, dec_seq - (CONV_W - 1):, :]
    x, h = _proj_out((p_p, p_s), w_out_bf, (xp, xs), row(norm_mix_post, 0),
                     row(norm_ffn_pre, 0), mp=mp, tm=TM_TAIL, sub=SUB_TAIL)

    a, w_down_bf = _ffn_up(h, ffn_w_gate, ffn_w_up, ffn_w_down, layer=0,
                           tm=TM_FFN, tn=TN_FFN, sub=SUB_FFN)
    x, h = _proj_out(a, w_down_bf, x, row(norm_ffn_post, 0),
                     row(norm_mix_pre, 1), mp=mp, tm=TM_DOWN, sub=SUB_DOWN)

    zz, w_out_bf = _sgu_in(h, b_w_in, b_b_in[:, None, :], b_w_out, layer=0,
                           tm=TM_FFN, tn=TN_SGU, sub=SUB_SGU)
    reps = SGU_CHUNK // dec_seq
    b_s = b_b_s[0]
    b_rows = jnp.stack([b_s, jnp.tile(b_s[:, :dec_seq], (1, reps))])
    bcol = jnp.swapaxes(b_rows, 1, 2)
    x, h, vn_s = _sgu_out(zz, b_ln_g[0].reshape(1, -1), b_ln_b[0].reshape(1, -1),
                          b_w_s, bcol, w_out_bf, x,
                          row(norm_mix_post, 1), row(norm_ffn_pre, 1),
                          layer=0, mp=mp, dec_seq=dec_seq, tm=TM_DOWN)
    state_sgu_v_sample = vn_s.reshape(1, dec_batch, dec_seq, d)

    a, w_down_bf = _ffn_up(h, ffn_w_gate, ffn_w_up, ffn_w_down, layer=1,
                           tm=TM_FFN, tn=TN_FFN, sub=SUB_FFN)
    yp, ys = _proj_out(a, w_down_bf, x, row(norm_ffn_post, 1),
                       None, mp=mp, tm=TM_DOWN, sub=SUB_DOWN)

    return (yp.reshape(batch, seq, d), ys.reshape(dec_batch, dec_seq, d),
            state_conv_prompt, state_conv_sample, state_sgu_v_sample)
```

```python
import functools
import math

import jax
import jax.numpy as jnp
from jax import lax
from jax.experimental import pallas as pl
from jax.experimental.pallas import tpu as pltpu

EPS = 1e-6
CONV_W = 3
SGU_CHUNK = 128
SGU_GROUP_DIM = 128
SUBLANES = 8
BF16_ROWS = 16

MIB = 1024 * 1024
VMEM_LIMIT_BYTES = 60 * MIB

F32 = jnp.float32
BF16 = jnp.bfloat16


def _params(n_axes):
    return pltpu.CompilerParams(
        dimension_semantics=("arbitrary",) * n_axes,
        vmem_limit_bytes=VMEM_LIMIT_BYTES)


def _rms(x, g):
    y = x * lax.rsqrt(jnp.mean(x * x, axis=-1, keepdims=True) + EPS)
    return y * g


def _dot(a, b):
    return jnp.dot(a, b, preferred_element_type=F32)


def _resident(block_shape, index_map):
    return pl.BlockSpec(block_shape, index_map, pipeline_mode=pl.Buffered(1))


def _round_specs(w, layer, n_inner, n_steps):
    k, d = w.shape[1:]
    assert k % n_steps == 0 and (k // n_steps) % BF16_ROWS == 0
    slab = k // n_steps
    step = lambda n, m: n * n_inner + m
    return (pl.BlockSpec((None, slab, d), lambda n, m: (layer, step(n, m), 0)),
            pl.BlockSpec((slab, d), lambda n, m: (step(n, m), 0)),
            jax.ShapeDtypeStruct((k, d), BF16))


def _conv_in_kernel(*refs, blocks_per_seq, dec_seq, sub, sample):
    if sample:
        (x_ref, g_ref, wb_ref, wc_ref, wz_ref, cw_ref, hist_ref,
         p_ref, state_ref, wbf_ref) = refs
    else:
        (x_ref, g_ref, wb_ref, wc_ref, wz_ref, cw_ref, wnext_ref,
         p_ref, state_ref, wnext_bf_ref, wbf_ref, carry_ref) = refs
        wnext_bf_ref[...] = wnext_ref[...].astype(BF16)
    m = pl.program_id(1)
    tm, tn = p_ref.shape

    @pl.when(m == 0)
    def _():
        wbf_ref[0] = wb_ref[...].astype(BF16)
        wbf_ref[1] = wc_ref[...].astype(BF16)
        wbf_ref[2] = wz_ref[...].astype(BF16)

    if not sample:
        @pl.when(m % blocks_per_seq == 0)
        def _():
            carry_ref[...] = jnp.zeros_like(carry_ref)

        prev = carry_ref[...]

    g = g_ref[...]
    cw = cw_ref[...]
    row = lax.broadcasted_iota(jnp.int32, (sub, tn), 0)
    for s in range(tm // sub):
        rows = pl.ds(s * sub, sub)
        h = _rms(x_ref[rows, :], g).astype(BF16)
        gate_b = _dot(h, wbf_ref[0])
        cz = _dot(h, wbf_ref[1]) * _dot(h, wbf_ref[2])

        if sample:
            hist = hist_ref[rows, :]
            pos = row % dec_seq
            cz1 = jnp.where(pos == 0, pltpu.roll(hist, sub - 1, 0), pltpu.roll(cz, 1, 0))
            cz2 = jnp.where(pos < CONV_W - 1, hist, pltpu.roll(cz, 2, 0))
            state_ref[rows, :] = cz
        else:
            prev1 = prev[SUBLANES - 1:SUBLANES, :]
            prev2 = prev[SUBLANES - 2:SUBLANES - 1, :]
            cz1 = jnp.where(row == 0, prev1, pltpu.roll(cz, 1, 0))
            cz2 = jnp.where(row == 0, prev2,
                            jnp.where(row == 1, prev1, pltpu.roll(cz, 2, 0)))
            prev = cz[sub - SUBLANES:, :]

        conv = cw[0:1, :] * cz2 + cw[1:2, :] * cz1 + cw[2:3, :] * cz
        p_ref[rows, :] = (gate_b * conv).astype(p_ref.dtype)

    if not sample:
        carry_ref[...] = prev
        state_ref[0] = prev


def _conv_in(x, g, w_in, conv_w, hist, w_next, *, layer, seq, dec_seq, tm, tn, sub):
    sample = hist is not None
    rows, d = x.shape
    nt = d // tn
    bps = seq // tm
    kernel = functools.partial(_conv_in_kernel, blocks_per_seq=bps,
                               dec_seq=dec_seq, sub=sub, sample=sample)
    row_tile = pl.BlockSpec((tm, tn), lambda n, m: (m, n))
    in_specs = [
        pl.BlockSpec((tm, d), lambda n, m: (m, 0)),
        pl.BlockSpec((1, d), lambda n, m: (0, 0)),
        pl.BlockSpec((None, d, tn), lambda n, m: (layer, 0, n)),
        pl.BlockSpec((None, d, tn), lambda n, m: (layer, 0, nt + n)),
        pl.BlockSpec((None, d, tn), lambda n, m: (layer, 0, 2 * nt + n)),
        pl.BlockSpec((CONV_W, tn), lambda n, m: (0, n)),
    ]
    scratch = [pltpu.VMEM((3, d, tn), BF16)]
    if sample:
        in_specs.append(row_tile)
        args = (x, g, w_in, w_in, w_in, conv_w, hist)
        state_spec = row_tile
        state_shape = jax.ShapeDtypeStruct((rows, d), F32)
    else:
        args = (x, g, w_in, w_in, w_in, conv_w, w_next)
        state_spec = pl.BlockSpec((1, SUBLANES, tn), lambda n, m: (m // bps, 0, n))
        state_shape = jax.ShapeDtypeStruct((rows // seq, SUBLANES, d), F32)
        scratch.append(pltpu.VMEM((SUBLANES, tn), F32))
        w_in_spec, w_out_spec, w_shape = _round_specs(w_next, layer, rows // tm,
                                                      nt * (rows // tm))
        in_specs.append(w_in_spec)
    out_specs = [row_tile, state_spec] + ([] if sample else [w_out_spec])
    out_shape = [jax.ShapeDtypeStruct((rows, d), BF16), state_shape]
    out_shape += [] if sample else [w_shape]
    return pl.pallas_call(
        kernel,
        grid=(nt, rows // tm),
        in_specs=in_specs,
        out_specs=out_specs,
        out_shape=out_shape,
        scratch_shapes=scratch,
        compiler_params=_params(2),
        name="conv_in_sample" if sample else "conv_in_prompt",
    )(*args)


def _proj_out_kernel(*refs, n_prompt_blocks, split_in, final, sub):
    m = pl.program_id(0)
    refs = list(refs)
    if split_in:
        ap_ref, as_ref, wbf_ref, xp_ref, xs_ref = refs[:5]
        refs = refs[5:]
    else:
        a_ref, wbf_ref, x_ref = refs[:3]
        refs = refs[3:]
    gpost_ref = refs[0]
    refs = refs[1:]
    if not final:
        gnext_ref = refs[0]
        refs = refs[1:]
    o0_ref, o1_ref = refs

    is_prompt = m < n_prompt_blocks

    def run(a_ref, x_ref, xo_ref, ho_ref):
        for s in range(a_ref.shape[0] // sub):
            rows = pl.ds(s * sub, sub)
            y = _dot(a_ref[rows, :], wbf_ref[...])
            xn = x_ref[rows, :] + _rms(y, gpost_ref[...])
            xo_ref[rows, :] = xn
            if ho_ref is not None:
                ho_ref[rows, :] = _rms(xn, gnext_ref[...]).astype(ho_ref.dtype)

    if split_in:
        pl.when(is_prompt)(lambda: run(ap_ref, xp_ref, o0_ref, o1_ref))
        pl.when(jnp.logical_not(is_prompt))(lambda: run(as_ref, xs_ref, o0_ref, o1_ref))
    elif final:
        pl.when(is_prompt)(lambda: run(a_ref, x_ref, o0_ref, None))
        pl.when(jnp.logical_not(is_prompt))(lambda: run(a_ref, x_ref, o1_ref, None))
    else:
        run(a_ref, x_ref, o0_ref, o1_ref)


def _proj_out(a, w, res, gpost, gnext, *, mp, tm, sub):
    final = gnext is None
    split_in = isinstance(a, tuple)
    assert split_in == isinstance(res, tuple)
    k, d = w.shape
    m_all = sum(t.shape[0] for t in a) if split_in else a.shape[0]
    ms = m_all - mp
    npb, nsb = mp // tm, ms // tm
    prompt_block = lambda m: (jnp.minimum(m, npb - 1), 0)
    sample_block = lambda m: (jnp.maximum(m - npb, 0), 0)
    row_block = lambda m: (m, 0)
    w_spec = _resident((k, d), lambda m: (0, 0))
    if split_in:
        in_specs = [pl.BlockSpec((tm, k), prompt_block), pl.BlockSpec((tm, k), sample_block),
                    w_spec,
                    pl.BlockSpec((tm, d), prompt_block), pl.BlockSpec((tm, d), sample_block)]
        args = (*a, w, *res)
    else:
        in_specs = [pl.BlockSpec((tm, k), row_block), w_spec,
                    pl.BlockSpec((tm, d), row_block)]
        args = (a, w, res)
    norm_args = (gpost,) if final else (gpost, gnext)
    in_specs += [_resident((1, d), lambda m: (0, 0)) for _ in norm_args]
    if final:
        out_specs = [pl.BlockSpec((tm, d), prompt_block),
                     pl.BlockSpec((tm, d), sample_block)]
        out_shape = [jax.ShapeDtypeStruct((mp, d), F32),
                     jax.ShapeDtypeStruct((ms, d), F32)]
    else:
        out_specs = [pl.BlockSpec((tm, d), row_block),
                     pl.BlockSpec((tm, d), row_block)]
        out_shape = [jax.ShapeDtypeStruct((m_all, d), F32),
                     jax.ShapeDtypeStruct((m_all, d), BF16)]
    kernel = functools.partial(_proj_out_kernel, n_prompt_blocks=npb,
                               split_in=split_in, final=final, sub=sub)
    return pl.pallas_call(
        kernel,
        grid=(npb + nsb,),
        in_specs=in_specs,
        out_specs=out_specs,
        out_shape=out_shape,
        compiler_params=_params(1),
        name="proj_out_k%d" % k,
    )(*args, *norm_args)


def _ffn_up_kernel(h_ref, wg_ref, wu_ref, wnext_ref, a_ref, wnext_bf_ref, wbf_ref, *, sub):
    wnext_bf_ref[...] = wnext_ref[...].astype(BF16)

    @pl.when(pl.program_id(1) == 0)
    def _():
        wbf_ref[0] = wg_ref[...].astype(BF16)
        wbf_ref[1] = wu_ref[...].astype(BF16)

    for s in range(a_ref.shape[0] // sub):
        rows = pl.ds(s * sub, sub)
        h = h_ref[rows, :]
        g = _dot(h, wbf_ref[0])
        u = _dot(h, wbf_ref[1])
        a_ref[rows, :] = (g * jax.nn.sigmoid(g) * u).astype(a_ref.dtype)


def _ffn_up(h, w_gate, w_up, w_next, *, layer, tm, tn, sub):
    m_all, d = h.shape
    f = w_gate.shape[2]
    grid = (f // tn, m_all // tm)
    w_in_spec, w_out_spec, w_shape = _round_specs(w_next, layer, grid[1], grid[0] * grid[1])
    return pl.pallas_call(
        functools.partial(_ffn_up_kernel, sub=sub),
        grid=grid,
        in_specs=[
            pl.BlockSpec((tm, d), lambda n, m: (m, 0)),
            pl.BlockSpec((None, d, tn), lambda n, m: (layer, 0, n)),
            pl.BlockSpec((None, d, tn), lambda n, m: (layer, 0, n)),
            w_in_spec,
        ],
        out_specs=[pl.BlockSpec((tm, tn), lambda n, m: (m, n)), w_out_spec],
        out_shape=[jax.ShapeDtypeStruct((m_all, f), BF16), w_shape],
        scratch_shapes=[pltpu.VMEM((2, d, tn), BF16)],
        compiler_params=_params(2),
        name="ffn_up",
    )(h, w_gate, w_up, w_next)


def _sgu_in_kernel(h_ref, w_ref, b_ref, wnext_ref, zz_ref, wnext_bf_ref, wbf_ref, *, sub):
    wnext_bf_ref[...] = wnext_ref[...].astype(BF16)

    @pl.when(pl.program_id(1) == 0)
    def _():
        wbf_ref[...] = w_ref[...].astype(BF16)

    for s in range(zz_ref.shape[0] // sub):
        rows = pl.ds(s * sub, sub)
        z = _dot(h_ref[rows, :], wbf_ref[...]) + b_ref[...]
        gelu = 0.5 * z * (1.0 + lax.erf(z * math.sqrt(0.5)))
        zz_ref[rows, :] = gelu.astype(zz_ref.dtype)


def _sgu_in(h, w, b, w_next, *, layer, tm, tn, sub):
    m_all, d = h.shape
    n_out = w.shape[2]
    grid = (n_out // tn, m_all // tm)
    w_in_spec, w_out_spec, w_shape = _round_specs(w_next, layer, grid[1], grid[0] * grid[1])
    return pl.pallas_call(
        functools.partial(_sgu_in_kernel, sub=sub),
        grid=grid,
        in_specs=[
            pl.BlockSpec((tm, d), lambda n, m: (m, 0)),
            pl.BlockSpec((None, d, tn), lambda n, m: (layer, 0, n)),
            pl.BlockSpec((None, 1, tn), lambda n, m: (layer, 0, n)),
            w_in_spec,
        ],
        out_specs=[pl.BlockSpec((tm, tn), lambda n, m: (m, n)), w_out_spec],
        out_shape=[jax.ShapeDtypeStruct((m_all, n_out), BF16), w_shape],
        scratch_shapes=[pltpu.VMEM((d, tn), BF16)],
        compiler_params=_params(2),
        name="sgu_in",
    )(h, w, b, w_next)


def _sgu_out_kernel(u_ref, v_ref, lng_ref, lnb_ref, ws_ref, bcol_ref, wbf_ref,
                    x_ref, gpost_ref, gnext_ref,
                    xo_ref, ho_ref, vn_ref,
                    mixw_ref, mixb_ref,
                    *, n_prompt_blocks, dec_seq, sub):
    m = pl.program_id(0)
    tm, d = x_ref.shape
    n_groups = d // SGU_GROUP_DIM

    def build_mixing(sample_kind):
        i = lax.broadcasted_iota(jnp.int32, (SGU_CHUNK, SGU_CHUNK), 0)
        j = lax.broadcasted_iota(jnp.int32, (SGU_CHUNK, SGU_CHUNK), 1)
        mask = i >= j
        if sample_kind:
            mask = mask & ((i // dec_seq) == (j // dec_seq))
        reps = SGU_CHUNK // dec_seq
        bcol = bcol_ref[int(sample_kind)]
        for g in range(n_groups):
            cols = slice(g * SGU_GROUP_DIM, (g + 1) * SGU_GROUP_DIM)
            if sample_kind:
                w = ws_ref[g, :dec_seq, :dec_seq]
                w = jnp.concatenate([w] * reps, axis=0)
                w = jnp.concatenate([w] * reps, axis=1)
            else:
                w = ws_ref[g]
            mixw_ref[g] = jnp.where(mask, w, 0.0).astype(BF16)
            mixb_ref[:, cols] = jnp.broadcast_to(bcol[:, g:g + 1],
                                                 (SGU_CHUNK, SGU_GROUP_DIM))

    pl.when(m == 0)(lambda: build_mixing(False))
    pl.when(m == n_prompt_blocks)(lambda: build_mixing(True))

    def gated_chunk(c):
        rows = pl.ds(c * SGU_CHUNK, SGU_CHUNK)
        v = v_ref[rows, :].astype(F32)
        mu = jnp.mean(v, axis=-1, keepdims=True)
        vc = v - mu
        vn = vc * lax.rsqrt(jnp.mean(vc * vc, axis=-1, keepdims=True) + EPS)
        vn = vn * lng_ref[...] + lnb_ref[...]
        vn_ref[rows, :] = vn
        vnb = vn.astype(BF16)

        gated = []
        for g in range(n_groups):
            cols = slice(g * SGU_GROUP_DIM, (g + 1) * SGU_GROUP_DIM)
            mixed = _dot(mixw_ref[g], vnb[:, cols]) + mixb_ref[:, cols]
            gated.append((u_ref[rows, cols].astype(F32) * mixed).astype(BF16))
        return jnp.concatenate(gated, axis=1)

    for s in range(tm // sub):
        chunks = range(s * sub // SGU_CHUNK, (s + 1) * sub // SGU_CHUNK)
        y = _dot(jnp.concatenate([gated_chunk(c) for c in chunks], axis=0), wbf_ref[...])
        rows = pl.ds(s * sub, sub)
        xn = x_ref[rows, :] + _rms(y, gpost_ref[...])
        xo_ref[rows, :] = xn
        ho_ref[rows, :] = _rms(xn, gnext_ref[...]).astype(ho_ref.dtype)


def _sgu_out(zz, ln_g, ln_b, w_s, bcol, w_out, x, gpost, gnext,
             *, layer, mp, dec_seq, tm, sub):
    m_all, d = x.shape
    ms = m_all - mp
    npb, nsb = mp // tm, ms // tm
    n_groups = d // SGU_GROUP_DIM
    row_block = lambda m: (m, 0)
    const2 = lambda m: (0, 0)
    kernel = functools.partial(_sgu_out_kernel, n_prompt_blocks=npb, dec_seq=dec_seq,
                               sub=sub)
    return pl.pallas_call(
        kernel,
        grid=(npb + nsb,),
        in_specs=[
            pl.BlockSpec((tm, d), lambda m: (m, 0)),
            pl.BlockSpec((tm, d), lambda m: (m, 1)),
            _resident((1, d), const2),
            _resident((1, d), const2),
            _resident((None, n_groups, SGU_CHUNK, SGU_CHUNK),
                      lambda m: (layer, 0, 0, 0)),
            _resident((2, SGU_CHUNK, n_groups), lambda m: (0, 0, 0)),
            _resident((d, d), const2),
            pl.BlockSpec((tm, d), row_block),
            _resident((1, d), const2),
            _resident((1, d), const2),
        ],
        out_specs=[
            pl.BlockSpec((tm, d), row_block),
            pl.BlockSpec((tm, d), row_block),
            pl.BlockSpec((tm, d), lambda m: (jnp.maximum(m - npb, 0), 0)),
        ],
        out_shape=[
            jax.ShapeDtypeStruct((m_all, d), F32),
            jax.ShapeDtypeStruct((m_all, d), BF16),
            jax.ShapeDtypeStruct((ms, d), F32),
        ],
        scratch_shapes=[pltpu.VMEM((n_groups, SGU_CHUNK, SGU_CHUNK), BF16),
                        pltpu.VMEM((SGU_CHUNK, d), F32)],
        compiler_params=_params(1),
        name="sgu_out",
    )(zz, zz, ln_g, ln_b, w_s, bcol, w_out, x, gpost, gnext)


TM_CONV = 1024
TM_SAMPLE = 512
SUB_CONV = 256
TM_FFN = 2176
SUB_FFN = 272
SUB_SGU = 1088
TM_TAIL = 512
SUB_TAIL = 256
TM_DOWN = 512
SUB_DOWN = 256
TN_CONV = 512
TN_FFN = 512
TN_SGU = 1024


def kernel(x_prompt, x_sample, cache_conv, norm_mix_pre, norm_mix_post, norm_ffn_pre, norm_ffn_post, a_w_in, a_conv_w, a_w_out, b_w_in, b_b_in, b_ln_g, b_ln_b, b_w_s, b_b_s, b_w_out, ffn_w_gate, ffn_w_up, ffn_w_down):
    batch, seq, d = x_prompt.shape
    dec_batch, dec_seq, _ = x_sample.shape
    depth = norm_mix_pre.shape[0]
    mp, ms = batch * seq, dec_batch * dec_seq
    assert depth == 2 and seq % TM_CONV == 0 and ms % TM_SAMPLE == 0
    assert SGU_CHUNK % dec_seq == 0 and SUB_CONV % dec_seq == 0 and dec_seq >= CONV_W - 1
    assert mp % TM_TAIL == 0 and ms % TM_TAIL == 0
    assert seq % SGU_CHUNK == 0 and SUB_DOWN % SGU_CHUNK == 0 and ms % TM_DOWN == 0
    assert (mp + ms) % TM_FFN == 0

    xp = x_prompt.reshape(mp, d)
    xs = x_sample.reshape(ms, d)
    row = lambda p, i: p[i].reshape(1, -1)

    conv_args = dict(layer=0, seq=seq, dec_seq=dec_seq, tn=TN_CONV, sub=SUB_CONV)
    p_p, tails, w_out_bf = _conv_in(xp, row(norm_mix_pre, 0), a_w_in, a_conv_w[0], None,
                                    a_w_out, tm=TM_CONV, **conv_args)
    hist = jnp.pad(cache_conv[0], ((0, 0), (0, dec_seq - (CONV_W - 1)), (0, 0)))
    p_s, cz_s = _conv_in(xs, row(norm_mix_pre, 0), a_w_in, a_conv_w[0],
                         hist.reshape(ms, d), None, tm=TM_SAMPLE, **conv_args)
    state_conv_prompt = tails[None, :, SUBLANES - (CONV_W - 1):, :]
    state_conv_sample = cz_s.reshape(dec_batch, dec_seq, d)[None, :, dec_seq - (CONV_W - 1):, :]
    x, h = _proj_out((p_p, p_s), w_out_bf, (xp, xs), row(norm_mix_post, 0),
                     row(norm_ffn_pre, 0), mp=mp, tm=TM_TAIL, sub=SUB_TAIL)

    a, w_down_bf = _ffn_up(h, ffn_w_gate, ffn_w_up, ffn_w_down, layer=0,
                           tm=TM_FFN, tn=TN_FFN, sub=SUB_FFN)
    x, h = _proj_out(a, w_down_bf, x, row(norm_ffn_post, 0),
                     row(norm_mix_pre, 1), mp=mp, tm=TM_DOWN, sub=SUB_DOWN)

    zz, w_out_bf = _sgu_in(h, b_w_in, b_b_in[:, None, :], b_w_out, layer=0,
                           tm=TM_FFN, tn=TN_SGU, sub=SUB_SGU)
    reps = SGU_CHUNK // dec_seq
    b_s = b_b_s[0]
    b_rows = jnp.stack([b_s, jnp.tile(b_s[:, :dec_seq], (1, reps))])
    bcol = jnp.swapaxes(b_rows, 1, 2)
    x, h, vn_s = _sgu_out(zz, b_ln_g[0].reshape(1, -1), b_ln_b[0].reshape(1, -1),
                          b_w_s, bcol, w_out_bf, x,
                          row(norm_mix_post, 1), row(norm_ffn_pre, 1),
                          layer=0, mp=mp, dec_seq=dec_seq, tm=TM_DOWN, sub=SUB_DOWN)
    state_sgu_v_sample = vn_s.reshape(1, dec_batch, dec_seq, d)

    a, w_down_bf = _ffn_up(h, ffn_w_gate, ffn_w_up, ffn_w_down, layer=1,
                           tm=TM_FFN, tn=TN_FFN, sub=SUB_FFN)
    yp, ys = _proj_out(a, w_down_bf, x, row(norm_ffn_post, 1),
                       None, mp=mp, tm=TM_DOWN, sub=SUB_DOWN)

    return (yp.reshape(batch, seq, d), ys.reshape(dec_batch, dec_seq, d),
            state_conv_prompt, state_conv_sample, state_sgu_v_sample)
```

```python
import functools
import math

import jax
import jax.numpy as jnp
from jax import lax
from jax.experimental import pallas as pl
from jax.experimental.pallas import tpu as pltpu

EPS = 1e-6
CONV_W = 3
SGU_CHUNK = 128
SGU_GROUP_DIM = 128
SUBLANES = 8
BF16_ROWS = 16

MIB = 1024 * 1024
VMEM_LIMIT_BYTES = 60 * MIB

F32 = jnp.float32
BF16 = jnp.bfloat16


def _params(n_axes):
    return pltpu.CompilerParams(
        dimension_semantics=("arbitrary",) * n_axes,
        vmem_limit_bytes=VMEM_LIMIT_BYTES)


def _rms(x, g):
    y = x * lax.rsqrt(jnp.mean(x * x, axis=-1, keepdims=True) + EPS)
    return y * g


def _dot(a, b):
    return jnp.dot(a, b, preferred_element_type=F32)


def _resident(block_shape, index_map):
    return pl.BlockSpec(block_shape, index_map, pipeline_mode=pl.Buffered(1))


def _round_specs(w, layer, n_inner, n_steps):
    k, d = w.shape[1:]
    assert k % n_steps == 0 and (k // n_steps) % BF16_ROWS == 0
    slab = k // n_steps
    step = lambda n, m: n * n_inner + m
    return (pl.BlockSpec((None, slab, d), lambda n, m: (layer, step(n, m), 0)),
            pl.BlockSpec((slab, d), lambda n, m: (step(n, m), 0)),
            jax.ShapeDtypeStruct((k, d), BF16))


def _conv_in_kernel(*refs, blocks_per_seq, dec_seq, sub, sample):
    if sample:
        (x_ref, g_ref, wb_ref, wc_ref, wz_ref, cw_ref, hist_ref,
         p_ref, state_ref, wbf_ref) = refs
    else:
        (x_ref, g_ref, wb_ref, wc_ref, wz_ref, cw_ref, wnext_ref,
         p_ref, state_ref, wnext_bf_ref, wbf_ref, carry_ref) = refs
        wnext_bf_ref[...] = wnext_ref[...].astype(BF16)
    m = pl.program_id(1)
    tm, tn = p_ref.shape

    @pl.when(m == 0)
    def _():
        wbf_ref[0] = wb_ref[...].astype(BF16)
        wbf_ref[1] = wc_ref[...].astype(BF16)
        wbf_ref[2] = wz_ref[...].astype(BF16)

    if not sample:
        @pl.when(m % blocks_per_seq == 0)
        def _():
            carry_ref[...] = jnp.zeros_like(carry_ref)

        prev = carry_ref[...]

    g = g_ref[...]
    cw = cw_ref[...]
    row = lax.broadcasted_iota(jnp.int32, (sub, tn), 0)
    for s in range(tm // sub):
        rows = pl.ds(s * sub, sub)
        h = _rms(x_ref[rows, :], g).astype(BF16)
        gate_b = _dot(h, wbf_ref[0])
        cz = _dot(h, wbf_ref[1]) * _dot(h, wbf_ref[2])

        if sample:
            hist = hist_ref[rows, :]
            pos = row % dec_seq
            cz1 = jnp.where(pos == 0, pltpu.roll(hist, sub - 1, 0), pltpu.roll(cz, 1, 0))
            cz2 = jnp.where(pos < CONV_W - 1, hist, pltpu.roll(cz, 2, 0))
            state_ref[rows, :] = cz
        else:
            prev1 = prev[SUBLANES - 1:SUBLANES, :]
            prev2 = prev[SUBLANES - 2:SUBLANES - 1, :]
            cz1 = jnp.where(row == 0, prev1, pltpu.roll(cz, 1, 0))
            cz2 = jnp.where(row == 0, prev2,
                            jnp.where(row == 1, prev1, pltpu.roll(cz, 2, 0)))
            prev = cz[sub - SUBLANES:, :]

        conv = cw[0:1, :] * cz2 + cw[1:2, :] * cz1 + cw[2:3, :] * cz
        p_ref[rows, :] = (gate_b * conv).astype(p_ref.dtype)

    if not sample:
        carry_ref[...] = prev
        state_ref[0] = prev


def _conv_in(x, g, w_in, conv_w, hist, w_next, *, layer, seq, dec_seq, tm, tn, sub):
    sample = hist is not None
    rows, d = x.shape
    nt = d // tn
    bps = seq // tm
    kernel = functools.partial(_conv_in_kernel, blocks_per_seq=bps,
                               dec_seq=dec_seq, sub=sub, sample=sample)
    row_tile = pl.BlockSpec((tm, tn), lambda n, m: (m, n))
    in_specs = [
        pl.BlockSpec((tm, d), lambda n, m: (m, 0)),
        pl.BlockSpec((1, d), lambda n, m: (0, 0)),
        pl.BlockSpec((None, d, tn), lambda n, m: (layer, 0, n)),
        pl.BlockSpec((None, d, tn), lambda n, m: (layer, 0, nt + n)),
        pl.BlockSpec((None, d, tn), lambda n, m: (layer, 0, 2 * nt + n)),
        pl.BlockSpec((CONV_W, tn), lambda n, m: (0, n)),
    ]
    scratch = [pltpu.VMEM((3, d, tn), BF16)]
    if sample:
        in_specs.append(row_tile)
        args = (x, g, w_in, w_in, w_in, conv_w, hist)
        state_spec = row_tile
        state_shape = jax.ShapeDtypeStruct((rows, d), F32)
    else:
        args = (x, g, w_in, w_in, w_in, conv_w, w_next)
        state_spec = pl.BlockSpec((1, SUBLANES, tn), lambda n, m: (m // bps, 0, n))
        state_shape = jax.ShapeDtypeStruct((rows // seq, SUBLANES, d), F32)
        scratch.append(pltpu.VMEM((SUBLANES, tn), F32))
        w_in_spec, w_out_spec, w_shape = _round_specs(w_next, layer, rows // tm,
                                                      nt * (rows // tm))
        in_specs.append(w_in_spec)
    out_specs = [row_tile, state_spec] + ([] if sample else [w_out_spec])
    out_shape = [jax.ShapeDtypeStruct((rows, d), BF16), state_shape]
    out_shape += [] if sample else [w_shape]
    return pl.pallas_call(
        kernel,
        grid=(nt, rows // tm),
        in_specs=in_specs,
        out_specs=out_specs,
        out_shape=out_shape,
        scratch_shapes=scratch,
        compiler_params=_params(2),
        name="conv_in_sample" if sample else "conv_in_prompt",
    )(*args)


def _proj_out_kernel(*refs, n_prompt_blocks, split_in, final, sub):
    m = pl.program_id(0)
    refs = list(refs)
    if split_in:
        ap_ref, as_ref, wbf_ref, xp_ref, xs_ref = refs[:5]
        refs = refs[5:]
    else:
        a_ref, wbf_ref, x_ref = refs[:3]
        refs = refs[3:]
    gpost_ref = refs[0]
    refs = refs[1:]
    if not final:
        gnext_ref = refs[0]
        refs = refs[1:]
    o0_ref, o1_ref = refs

    is_prompt = m < n_prompt_blocks

    def run(a_ref, x_ref, xo_ref, ho_ref):
        for s in range(a_ref.shape[0] // sub):
            rows = pl.ds(s * sub, sub)
            y = _dot(a_ref[rows, :], wbf_ref[...])
            xn = x_ref[rows, :] + _rms(y, gpost_ref[...])
            xo_ref[rows, :] = xn
            if ho_ref is not None:
                ho_ref[rows, :] = _rms(xn, gnext_ref[...]).astype(ho_ref.dtype)

    if split_in:
        pl.when(is_prompt)(lambda: run(ap_ref, xp_ref, o0_ref, o1_ref))
        pl.when(jnp.logical_not(is_prompt))(lambda: run(as_ref, xs_ref, o0_ref, o1_ref))
    elif final:
        pl.when(is_prompt)(lambda: run(a_ref, x_ref, o0_ref, None))
        pl.when(jnp.logical_not(is_prompt))(lambda: run(a_ref, x_ref, o1_ref, None))
    else:
        run(a_ref, x_ref, o0_ref, o1_ref)


def _proj_out(a, w, res, gpost, gnext, *, mp, tm, sub):
    final = gnext is None
    split_in = isinstance(a, tuple)
    assert split_in == isinstance(res, tuple)
    k, d = w.shape
    m_all = sum(t.shape[0] for t in a) if split_in else a.shape[0]
    ms = m_all - mp
    npb, nsb = mp // tm, ms // tm
    prompt_block = lambda m: (jnp.minimum(m, npb - 1), 0)
    sample_block = lambda m: (jnp.maximum(m - npb, 0), 0)
    row_block = lambda m: (m, 0)
    w_spec = _resident((k, d), lambda m: (0, 0))
    if split_in:
        in_specs = [pl.BlockSpec((tm, k), prompt_block), pl.BlockSpec((tm, k), sample_block),
                    w_spec,
                    pl.BlockSpec((tm, d), prompt_block), pl.BlockSpec((tm, d), sample_block)]
        args = (*a, w, *res)
    else:
        in_specs = [pl.BlockSpec((tm, k), row_block), w_spec,
                    pl.BlockSpec((tm, d), row_block)]
        args = (a, w, res)
    norm_args = (gpost,) if final else (gpost, gnext)
    in_specs += [_resident((1, d), lambda m: (0, 0)) for _ in norm_args]
    if final:
        out_specs = [pl.BlockSpec((tm, d), prompt_block),
                     pl.BlockSpec((tm, d), sample_block)]
        out_shape = [jax.ShapeDtypeStruct((mp, d), F32),
                     jax.ShapeDtypeStruct((ms, d), F32)]
    else:
        out_specs = [pl.BlockSpec((tm, d), row_block),
                     pl.BlockSpec((tm, d), row_block)]
        out_shape = [jax.ShapeDtypeStruct((m_all, d), F32),
                     jax.ShapeDtypeStruct((m_all, d), BF16)]
    kernel = functools.partial(_proj_out_kernel, n_prompt_blocks=npb,
                               split_in=split_in, final=final, sub=sub)
    return pl.pallas_call(
        kernel,
        grid=(npb + nsb,),
        in_specs=in_specs,
        out_specs=out_specs,
        out_shape=out_shape,
        compiler_params=_params(1),
        name="proj_out_k%d" % k,
    )(*args, *norm_args)


def _ffn_up_kernel(h_ref, wg_ref, wu_ref, wnext_ref, a_ref, wnext_bf_ref, wbf_ref, *, sub):
    wnext_bf_ref[...] = wnext_ref[...].astype(BF16)

    @pl.when(pl.program_id(1) == 0)
    def _():
        wbf_ref[0] = wg_ref[...].astype(BF16)
        wbf_ref[1] = wu_ref[...].astype(BF16)

    for s in range(a_ref.shape[0] // sub):
        rows = pl.ds(s * sub, sub)
        h = h_ref[rows, :]
        g = _dot(h, wbf_ref[0])
        u = _dot(h, wbf_ref[1])
        a_ref[rows, :] = (g * jax.nn.sigmoid(g) * u).astype(a_ref.dtype)


def _ffn_up(h, w_gate, w_up, w_next, *, layer, tm, tn, sub):
    m_all, d = h.shape
    f = w_gate.shape[2]
    grid = (f // tn, m_all // tm)
    w_in_spec, w_out_spec, w_shape = _round_specs(w_next, layer, grid[1], grid[0] * grid[1])
    return pl.pallas_call(
        functools.partial(_ffn_up_kernel, sub=sub),
        grid=grid,
        in_specs=[
            pl.BlockSpec((tm, d), lambda n, m: (m, 0)),
            pl.BlockSpec((None, d, tn), lambda n, m: (layer, 0, n)),
            pl.BlockSpec((None, d, tn), lambda n, m: (layer, 0, n)),
            w_in_spec,
        ],
        out_specs=[pl.BlockSpec((tm, tn), lambda n, m: (m, n)), w_out_spec],
        out_shape=[jax.ShapeDtypeStruct((m_all, f), BF16), w_shape],
        scratch_shapes=[pltpu.VMEM((2, d, tn), BF16)],
        compiler_params=_params(2),
        name="ffn_up",
    )(h, w_gate, w_up, w_next)


def _sgu_in_kernel(h_ref, w_ref, b_ref, wnext_ref, zz_ref, wnext_bf_ref, wbf_ref, *, sub):
    wnext_bf_ref[...] = wnext_ref[...].astype(BF16)

    @pl.when(pl.program_id(1) == 0)
    def _():
        wbf_ref[...] = w_ref[...].astype(BF16)

    for s in range(zz_ref.shape[0] // sub):
        rows = pl.ds(s * sub, sub)
        z = _dot(h_ref[rows, :], wbf_ref[...]) + b_ref[...]
        gelu = 0.5 * z * (1.0 + lax.erf(z * math.sqrt(0.5)))
        zz_ref[rows, :] = gelu.astype(zz_ref.dtype)


def _sgu_in(h, w, b, w_next, *, layer, tm, tn, sub):
    m_all, d = h.shape
    n_out = w.shape[2]
    grid = (n_out // tn, m_all // tm)
    w_in_spec, w_out_spec, w_shape = _round_specs(w_next, layer, grid[1], grid[0] * grid[1])
    return pl.pallas_call(
        functools.partial(_sgu_in_kernel, sub=sub),
        grid=grid,
        in_specs=[
            pl.BlockSpec((tm, d), lambda n, m: (m, 0)),
            pl.BlockSpec((None, d, tn), lambda n, m: (layer, 0, n)),
            pl.BlockSpec((None, 1, tn), lambda n, m: (layer, 0, n)),
            w_in_spec,
        ],
        out_specs=[pl.BlockSpec((tm, tn), lambda n, m: (m, n)), w_out_spec],
        out_shape=[jax.ShapeDtypeStruct((m_all, n_out), BF16), w_shape],
        scratch_shapes=[pltpu.VMEM((d, tn), BF16)],
        compiler_params=_params(2),
        name="sgu_in",
    )(h, w, b, w_next)


def _sgu_out_kernel(u_ref, v_ref, lng_ref, lnb_ref, ws_ref, bcol_ref, wbf_ref,
                    x_ref, gpost_ref, gnext_ref,
                    xo_ref, ho_ref, vn_ref,
                    mixw_ref, mixb_ref,
                    *, n_prompt_blocks, dec_seq, sub):
    m = pl.program_id(0)
    tm, d = x_ref.shape
    n_groups = d // SGU_GROUP_DIM

    def build_mixing(sample_kind):
        i = lax.broadcasted_iota(jnp.int32, (SGU_CHUNK, SGU_CHUNK), 0)
        j = lax.broadcasted_iota(jnp.int32, (SGU_CHUNK, SGU_CHUNK), 1)
        mask = i >= j
        if sample_kind:
            mask = mask & ((i // dec_seq) == (j // dec_seq))
        reps = SGU_CHUNK // dec_seq
        bcol = bcol_ref[int(sample_kind)]
        for g in range(n_groups):
            cols = slice(g * SGU_GROUP_DIM, (g + 1) * SGU_GROUP_DIM)
            if sample_kind:
                w = ws_ref[g, :dec_seq, :dec_seq]
                w = jnp.concatenate([w] * reps, axis=0)
                w = jnp.concatenate([w] * reps, axis=1)
            else:
                w = ws_ref[g]
            mixw_ref[g] = jnp.where(mask, w, 0.0).astype(BF16)
            mixb_ref[:, cols] = jnp.broadcast_to(bcol[:, g:g + 1],
                                                 (SGU_CHUNK, SGU_GROUP_DIM))

    pl.when(m == 0)(lambda: build_mixing(False))
    pl.when(m == n_prompt_blocks)(lambda: build_mixing(True))

    def gated_chunk(c):
        rows = pl.ds(c * SGU_CHUNK, SGU_CHUNK)
        v = v_ref[rows, :].astype(F32)
        mu = jnp.mean(v, axis=-1, keepdims=True)
        vc = v - mu
        vn = vc * lax.rsqrt(jnp.mean(vc * vc, axis=-1, keepdims=True) + EPS)
        vn = vn * lng_ref[...] + lnb_ref[...]
        vn_ref[rows, :] = vn
        vnb = vn.astype(BF16)

        gated = []
        for g in range(n_groups):
            cols = slice(g * SGU_GROUP_DIM, (g + 1) * SGU_GROUP_DIM)
            mixed = _dot(mixw_ref[g], vnb[:, cols]) + mixb_ref[:, cols]
            gated.append((u_ref[rows, cols].astype(F32) * mixed).astype(BF16))
        return jnp.concatenate(gated, axis=1)

    for s in range(tm // sub):
        chunks = range(s * sub // SGU_CHUNK, (s + 1) * sub // SGU_CHUNK)
        y = _dot(jnp.concatenate([gated_chunk(c) for c in chunks], axis=0), wbf_ref[...])
        rows = pl.ds(s * sub, sub)
        xn = x_ref[rows, :] + _rms(y, gpost_ref[...])
        xo_ref[rows, :] = xn
        ho_ref[rows, :] = _rms(xn, gnext_ref[...]).astype(ho_ref.dtype)


def _sgu_out(zz, ln_g, ln_b, w_s, bcol, w_out, x, gpost, gnext,
             *, layer, mp, dec_seq, tm, sub):
    m_all, d = x.shape
    ms = m_all - mp
    npb, nsb = mp // tm, ms // tm
    n_groups = d // SGU_GROUP_DIM
    row_block = lambda m: (m, 0)
    const2 = lambda m: (0, 0)
    kernel = functools.partial(_sgu_out_kernel, n_prompt_blocks=npb, dec_seq=dec_seq,
                               sub=sub)
    return pl.pallas_call(
        kernel,
        grid=(npb + nsb,),
        in_specs=[
            pl.BlockSpec((tm, d), lambda m: (m, 0)),
            pl.BlockSpec((tm, d), lambda m: (m, 1)),
            _resident((1, d), const2),
            _resident((1, d), const2),
            _resident((None, n_groups, SGU_CHUNK, SGU_CHUNK),
                      lambda m: (layer, 0, 0, 0)),
            _resident((2, SGU_CHUNK, n_groups), lambda m: (0, 0, 0)),
            _resident((d, d), const2),
            pl.BlockSpec((tm, d), row_block),
            _resident((1, d), const2),
            _resident((1, d), const2),
        ],
        out_specs=[
            pl.BlockSpec((tm, d), row_block),
            pl.BlockSpec((tm, d), row_block),
            pl.BlockSpec((tm, d), lambda m: (jnp.maximum(m - npb, 0), 0)),
        ],
        out_shape=[
            jax.ShapeDtypeStruct((m_all, d), F32),
            jax.ShapeDtypeStruct((m_all, d), BF16),
            jax.ShapeDtypeStruct((ms, d), F32),
        ],
        scratch_shapes=[pltpu.VMEM((n_groups, SGU_CHUNK, SGU_CHUNK), BF16),
                        pltpu.VMEM((SGU_CHUNK, d), F32)],
        compiler_params=_params(1),
        name="sgu_out",
    )(zz, zz, ln_g, ln_b, w_s, bcol, w_out, x, gpost, gnext)


TM_CONV = 1024
TM_SAMPLE = 512
SUB_CONV = 512
TM_FFN = 2176
SUB_FFN = 272
SUB_SGU = 1088
TM_TAIL = 512
SUB_TAIL = 256
TM_DOWN = 512
SUB_DOWN = 256
TN_CONV = 512
TN_FFN = 512
TN_SGU = 1024


def kernel(x_prompt, x_sample, cache_conv, norm_mix_pre, norm_mix_post, norm_ffn_pre, norm_ffn_post, a_w_in, a_conv_w, a_w_out, b_w_in, b_b_in, b_ln_g, b_ln_b, b_w_s, b_b_s, b_w_out, ffn_w_gate, ffn_w_up, ffn_w_down):
    batch, seq, d = x_prompt.shape
    dec_batch, dec_seq, _ = x_sample.shape
    depth = norm_mix_pre.shape[0]
    mp, ms = batch * seq, dec_batch * dec_seq
    assert depth == 2 and seq % TM_CONV == 0 and ms % TM_SAMPLE == 0
    assert SGU_CHUNK % dec_seq == 0 and SUB_CONV % dec_seq == 0 and dec_seq >= CONV_W - 1
    assert mp % TM_TAIL == 0 and ms % TM_TAIL == 0
    assert seq % SGU_CHUNK == 0 and SUB_DOWN % SGU_CHUNK == 0 and ms % TM_DOWN == 0
    assert (mp + ms) % TM_FFN == 0

    xp = x_prompt.reshape(mp, d)
    xs = x_sample.reshape(ms, d)
    row = lambda p, i: p[i].reshape(1, -1)

    conv_args = dict(layer=0, seq=seq, dec_seq=dec_seq, tn=TN_CONV, sub=SUB_CONV)
    p_p, tails, w_out_bf = _conv_in(xp, row(norm_mix_pre, 0), a_w_in, a_conv_w[0], None,
                                    a_w_out, tm=TM_CONV, **conv_args)
    hist = jnp.pad(cache_conv[0], ((0, 0), (0, dec_seq - (CONV_W - 1)), (0, 0)))
    p_s, cz_s = _conv_in(xs, row(norm_mix_pre, 0), a_w_in, a_conv_w[0],
                         hist.reshape(ms, d), None, tm=TM_SAMPLE, **conv_args)
    state_conv_prompt = tails[None, :, SUBLANES - (CONV_W - 1):, :]
    state_conv_sample = cz_s.reshape(dec_batch, dec_seq, d)[None, :, dec_seq - (CONV_W - 1):, :]
    x, h = _proj_out((p_p, p_s), w_out_bf, (xp, xs), row(norm_mix_post, 0),
                     row(norm_ffn_pre, 0), mp=mp, tm=TM_TAIL, sub=SUB_TAIL)

    a, w_down_bf = _ffn_up(h, ffn_w_gate, ffn_w_up, ffn_w_down, layer=0,
                           tm=TM_FFN, tn=TN_FFN, sub=SUB_FFN)
    x, h = _proj_out(a, w_down_bf, x, row(norm_ffn_post, 0),
                     row(norm_mix_pre, 1), mp=mp, tm=TM_DOWN, sub=SUB_DOWN)

    zz, w_out_bf = _sgu_in(h, b_w_in, b_b_in[:, None, :], b_w_out, layer=0,
                           tm=TM_FFN, tn=TN_SGU, sub=SUB_SGU)
    reps = SGU_CHUNK // dec_seq
    b_s = b_b_s[0]
    b_rows = jnp.stack([b_s, jnp.tile(b_s[:, :dec_seq], (1, reps))])
    bcol = jnp.swapaxes(b_rows, 1, 2)
    x, h, vn_s = _sgu_out(zz, b_ln_g[0].reshape(1, -1), b_ln_b[0].reshape(1, -1),
                          b_w_s, bcol, w_out_bf, x,
                          row(norm_mix_post, 1), row(norm_ffn_pre, 1),
                          layer=0, mp=mp, dec_seq=dec_seq, tm=TM_DOWN, sub=SUB_DOWN)
    state_sgu_v_sample = vn_s.reshape(1, dec_batch, dec_seq, d)

    a, w_down_bf = _ffn_up(h, ffn_w_gate, ffn_w_up, ffn_w_down, layer=1,
                           tm=TM_FFN, tn=TN_FFN, sub=SUB_FFN)
    yp, ys = _proj_out(a, w_down_bf, x, row(norm_ffn_post, 1),
                       None, mp=mp, tm=TM_DOWN, sub=SUB_DOWN)

    return (yp.reshape(batch, seq, d), ys.reshape(dec_batch, dec_seq, d),
            state_conv_prompt, state_conv_sample, state_sgu_v_sample)
```

```python
import functools
import math

import jax
import jax.numpy as jnp
from jax import lax
from jax.experimental import pallas as pl
from jax.experimental.pallas import tpu as pltpu

EPS = 1e-6
CONV_W = 3
SGU_CHUNK = 128
SGU_GROUP_DIM = 128
SUBLANES = 8
BF16_ROWS = 16

MIB = 1024 * 1024
VMEM_LIMIT_BYTES = 60 * MIB

F32 = jnp.float32
BF16 = jnp.bfloat16


def _params(n_axes):
    return pltpu.CompilerParams(
        dimension_semantics=("arbitrary",) * n_axes,
        vmem_limit_bytes=VMEM_LIMIT_BYTES)


def _rms(x, g):
    y = x * lax.rsqrt(jnp.mean(x * x, axis=-1, keepdims=True) + EPS)
    return y * g


def _dot(a, b):
    return jnp.dot(a, b, preferred_element_type=F32)


def _resident(block_shape, index_map):
    return pl.BlockSpec(block_shape, index_map, pipeline_mode=pl.Buffered(1))


def _round_specs(w, layer, n_inner, n_steps):
    k, d = w.shape[1:]
    assert k % n_steps == 0 and (k // n_steps) % BF16_ROWS == 0
    slab = k // n_steps
    step = lambda n, m: n * n_inner + m
    return (pl.BlockSpec((None, slab, d), lambda n, m: (layer, step(n, m), 0)),
            pl.BlockSpec((slab, d), lambda n, m: (step(n, m), 0)),
            jax.ShapeDtypeStruct((k, d), BF16))


def _conv_in_kernel(*refs, blocks_per_seq, dec_seq, sub, sample):
    if sample:
        (x_ref, g_ref, wb_ref, wc_ref, wz_ref, cw_ref, hist_ref,
         p_ref, state_ref, wbf_ref) = refs
    else:
        (x_ref, g_ref, wb_ref, wc_ref, wz_ref, cw_ref, wnext_ref,
         p_ref, state_ref, wnext_bf_ref, wbf_ref, carry_ref) = refs
        wnext_bf_ref[...] = wnext_ref[...].astype(BF16)
    m = pl.program_id(1)
    tm, tn = p_ref.shape

    @pl.when(m == 0)
    def _():
        wbf_ref[0] = wb_ref[...].astype(BF16)
        wbf_ref[1] = wc_ref[...].astype(BF16)
        wbf_ref[2] = wz_ref[...].astype(BF16)

    if not sample:
        @pl.when(m % blocks_per_seq == 0)
        def _():
            carry_ref[...] = jnp.zeros_like(carry_ref)

        prev = carry_ref[...]

    g = g_ref[...]
    cw = cw_ref[...]
    row = lax.broadcasted_iota(jnp.int32, (sub, tn), 0)
    for s in range(tm // sub):
        rows = pl.ds(s * sub, sub)
        h = _rms(x_ref[rows, :], g).astype(BF16)
        gate_b = _dot(h, wbf_ref[0])
        cz = _dot(h, wbf_ref[1]) * _dot(h, wbf_ref[2])

        if sample:
            hist = hist_ref[rows, :]
            pos = row % dec_seq
            cz1 = jnp.where(pos == 0, pltpu.roll(hist, sub - 1, 0), pltpu.roll(cz, 1, 0))
            cz2 = jnp.where(pos < CONV_W - 1, hist, pltpu.roll(cz, 2, 0))
            state_ref[rows, :] = cz
        else:
            prev1 = prev[SUBLANES - 1:SUBLANES, :]
            prev2 = prev[SUBLANES - 2:SUBLANES - 1, :]
            cz1 = jnp.where(row == 0, prev1, pltpu.roll(cz, 1, 0))
            cz2 = jnp.where(row == 0, prev2,
                            jnp.where(row == 1, prev1, pltpu.roll(cz, 2, 0)))
            prev = cz[sub - SUBLANES:, :]

        conv = cw[0:1, :] * cz2 + cw[1:2, :] * cz1 + cw[2:3, :] * cz
        p_ref[rows, :] = (gate_b * conv).astype(p_ref.dtype)

    if not sample:
        carry_ref[...] = prev
        state_ref[0] = prev


def _conv_in(x, g, w_in, conv_w, hist, w_next, *, layer, seq, dec_seq, tm, tn, sub):
    sample = hist is not None
    rows, d = x.shape
    nt = d // tn
    bps = seq // tm
    kernel = functools.partial(_conv_in_kernel, blocks_per_seq=bps,
                               dec_seq=dec_seq, sub=sub, sample=sample)
    row_tile = pl.BlockSpec((tm, tn), lambda n, m: (m, n))
    in_specs = [
        pl.BlockSpec((tm, d), lambda n, m: (m, 0)),
        pl.BlockSpec((1, d), lambda n, m: (0, 0)),
        pl.BlockSpec((None, d, tn), lambda n, m: (layer, 0, n)),
        pl.BlockSpec((None, d, tn), lambda n, m: (layer, 0, nt + n)),
        pl.BlockSpec((None, d, tn), lambda n, m: (layer, 0, 2 * nt + n)),
        pl.BlockSpec((CONV_W, tn), lambda n, m: (0, n)),
    ]
    scratch = [pltpu.VMEM((3, d, tn), BF16)]
    if sample:
        in_specs.append(row_tile)
        args = (x, g, w_in, w_in, w_in, conv_w, hist)
        state_spec = row_tile
        state_shape = jax.ShapeDtypeStruct((rows, d), F32)
    else:
        args = (x, g, w_in, w_in, w_in, conv_w, w_next)
        state_spec = pl.BlockSpec((1, SUBLANES, tn), lambda n, m: (m // bps, 0, n))
        state_shape = jax.ShapeDtypeStruct((rows // seq, SUBLANES, d), F32)
        scratch.append(pltpu.VMEM((SUBLANES, tn), F32))
        w_in_spec, w_out_spec, w_shape = _round_specs(w_next, layer, rows // tm,
                                                      nt * (rows // tm))
        in_specs.append(w_in_spec)
    out_specs = [row_tile, state_spec] + ([] if sample else [w_out_spec])
    out_shape = [jax.ShapeDtypeStruct((rows, d), BF16), state_shape]
    out_shape += [] if sample else [w_shape]
    return pl.pallas_call(
        kernel,
        grid=(nt, rows // tm),
        in_specs=in_specs,
        out_specs=out_specs,
        out_shape=out_shape,
        scratch_shapes=scratch,
        compiler_params=_params(2),
        name="conv_in_sample" if sample else "conv_in_prompt",
    )(*args)


def _norm_tail(y, row0, x_ref, xo_ref, ho_ref, gpost_ref, gnext_ref):
    for r in range(0, y.shape[0], TAIL_ROWS):
        piece = pl.ds(row0 + r, TAIL_ROWS)
        xn = x_ref[piece, :] + _rms(y[r:r + TAIL_ROWS, :], gpost_ref[...])
        xo_ref[piece, :] = xn
        if ho_ref is not None:
            ho_ref[piece, :] = _rms(xn, gnext_ref[...]).astype(ho_ref.dtype)


def _proj_out_kernel(*refs, n_prompt_blocks, split_in, final, sub):
    m = pl.program_id(0)
    refs = list(refs)
    if split_in:
        ap_ref, as_ref, wbf_ref, xp_ref, xs_ref = refs[:5]
        refs = refs[5:]
    else:
        a_ref, wbf_ref, x_ref = refs[:3]
        refs = refs[3:]
    gpost_ref = refs[0]
    refs = refs[1:]
    if not final:
        gnext_ref = refs[0]
        refs = refs[1:]
    o0_ref, o1_ref = refs

    is_prompt = m < n_prompt_blocks

    def run(a_ref, x_ref, xo_ref, ho_ref):
        for s in range(a_ref.shape[0] // sub):
            rows = pl.ds(s * sub, sub)
            y = _dot(a_ref[rows, :], wbf_ref[...])
            _norm_tail(y, s * sub, x_ref, xo_ref, ho_ref, gpost_ref,
                       None if ho_ref is None else gnext_ref)

    if split_in:
        pl.when(is_prompt)(lambda: run(ap_ref, xp_ref, o0_ref, o1_ref))
        pl.when(jnp.logical_not(is_prompt))(lambda: run(as_ref, xs_ref, o0_ref, o1_ref))
    elif final:
        pl.when(is_prompt)(lambda: run(a_ref, x_ref, o0_ref, None))
        pl.when(jnp.logical_not(is_prompt))(lambda: run(a_ref, x_ref, o1_ref, None))
    else:
        run(a_ref, x_ref, o0_ref, o1_ref)


def _proj_out(a, w, res, gpost, gnext, *, mp, tm, sub):
    final = gnext is None
    split_in = isinstance(a, tuple)
    assert split_in == isinstance(res, tuple)
    k, d = w.shape
    m_all = sum(t.shape[0] for t in a) if split_in else a.shape[0]
    ms = m_all - mp
    npb, nsb = mp // tm, ms // tm
    prompt_block = lambda m: (jnp.minimum(m, npb - 1), 0)
    sample_block = lambda m: (jnp.maximum(m - npb, 0), 0)
    row_block = lambda m: (m, 0)
    w_spec = _resident((k, d), lambda m: (0, 0))
    if split_in:
        in_specs = [pl.BlockSpec((tm, k), prompt_block), pl.BlockSpec((tm, k), sample_block),
                    w_spec,
                    pl.BlockSpec((tm, d), prompt_block), pl.BlockSpec((tm, d), sample_block)]
        args = (*a, w, *res)
    else:
        in_specs = [pl.BlockSpec((tm, k), row_block), w_spec,
                    pl.BlockSpec((tm, d), row_block)]
        args = (a, w, res)
    norm_args = (gpost,) if final else (gpost, gnext)
    in_specs += [_resident((1, d), lambda m: (0, 0)) for _ in norm_args]
    if final:
        out_specs = [pl.BlockSpec((tm, d), prompt_block),
                     pl.BlockSpec((tm, d), sample_block)]
        out_shape = [jax.ShapeDtypeStruct((mp, d), F32),
                     jax.ShapeDtypeStruct((ms, d), F32)]
    else:
        out_specs = [pl.BlockSpec((tm, d), row_block),
                     pl.BlockSpec((tm, d), row_block)]
        out_shape = [jax.ShapeDtypeStruct((m_all, d), F32),
                     jax.ShapeDtypeStruct((m_all, d), BF16)]
    kernel = functools.partial(_proj_out_kernel, n_prompt_blocks=npb,
                               split_in=split_in, final=final, sub=sub)
    return pl.pallas_call(
        kernel,
        grid=(npb + nsb,),
        in_specs=in_specs,
        out_specs=out_specs,
        out_shape=out_shape,
        compiler_params=_params(1),
        name="proj_out_k%d" % k,
    )(*args, *norm_args)


def _ffn_up_kernel(h_ref, wg_ref, wu_ref, wnext_ref, a_ref, wnext_bf_ref, wbf_ref, *, sub):
    wnext_bf_ref[...] = wnext_ref[...].astype(BF16)

    @pl.when(pl.program_id(1) == 0)
    def _():
        wbf_ref[0] = wg_ref[...].astype(BF16)
        wbf_ref[1] = wu_ref[...].astype(BF16)

    for s in range(a_ref.shape[0] // sub):
        rows = pl.ds(s * sub, sub)
        h = h_ref[rows, :]
        g = _dot(h, wbf_ref[0])
        u = _dot(h, wbf_ref[1])
        a_ref[rows, :] = (g * jax.nn.sigmoid(g) * u).astype(a_ref.dtype)


def _ffn_up(h, w_gate, w_up, w_next, *, layer, tm, tn, sub):
    m_all, d = h.shape
    f = w_gate.shape[2]
    grid = (f // tn, m_all // tm)
    w_in_spec, w_out_spec, w_shape = _round_specs(w_next, layer, grid[1], grid[0] * grid[1])
    return pl.pallas_call(
        functools.partial(_ffn_up_kernel, sub=sub),
        grid=grid,
        in_specs=[
            pl.BlockSpec((tm, d), lambda n, m: (m, 0)),
            pl.BlockSpec((None, d, tn), lambda n, m: (layer, 0, n)),
            pl.BlockSpec((None, d, tn), lambda n, m: (layer, 0, n)),
            w_in_spec,
        ],
        out_specs=[pl.BlockSpec((tm, tn), lambda n, m: (m, n)), w_out_spec],
        out_shape=[jax.ShapeDtypeStruct((m_all, f), BF16), w_shape],
        scratch_shapes=[pltpu.VMEM((2, d, tn), BF16)],
        compiler_params=_params(2),
        name="ffn_up",
    )(h, w_gate, w_up, w_next)


def _sgu_in_kernel(h_ref, w_ref, b_ref, wnext_ref, zz_ref, wnext_bf_ref, wbf_ref, *, sub):
    wnext_bf_ref[...] = wnext_ref[...].astype(BF16)

    @pl.when(pl.program_id(1) == 0)
    def _():
        wbf_ref[...] = w_ref[...].astype(BF16)

    for s in range(zz_ref.shape[0] // sub):
        rows = pl.ds(s * sub, sub)
        z = _dot(h_ref[rows, :], wbf_ref[...]) + b_ref[...]
        gelu = 0.5 * z * (1.0 + lax.erf(z * math.sqrt(0.5)))
        zz_ref[rows, :] = gelu.astype(zz_ref.dtype)


def _sgu_in(h, w, b, w_next, *, layer, tm, tn, sub):
    m_all, d = h.shape
    n_out = w.shape[2]
    grid = (n_out // tn, m_all // tm)
    w_in_spec, w_out_spec, w_shape = _round_specs(w_next, layer, grid[1], grid[0] * grid[1])
    return pl.pallas_call(
        functools.partial(_sgu_in_kernel, sub=sub),
        grid=grid,
        in_specs=[
            pl.BlockSpec((tm, d), lambda n, m: (m, 0)),
            pl.BlockSpec((None, d, tn), lambda n, m: (layer, 0, n)),
            pl.BlockSpec((None, 1, tn), lambda n, m: (layer, 0, n)),
            w_in_spec,
        ],
        out_specs=[pl.BlockSpec((tm, tn), lambda n, m: (m, n)), w_out_spec],
        out_shape=[jax.ShapeDtypeStruct((m_all, n_out), BF16), w_shape],
        scratch_shapes=[pltpu.VMEM((d, tn), BF16)],
        compiler_params=_params(2),
        name="sgu_in",
    )(h, w, b, w_next)


def _sgu_out_kernel(u_ref, v_ref, lng_ref, lnb_ref, ws_ref, bcol_ref, wbf_ref,
                    x_ref, gpost_ref, gnext_ref,
                    xo_ref, ho_ref, vn_ref,
                    mixw_ref, mixb_ref,
                    *, n_prompt_blocks, dec_seq, sub):
    m = pl.program_id(0)
    tm, d = x_ref.shape
    n_groups = d // SGU_GROUP_DIM

    def build_mixing(sample_kind):
        i = lax.broadcasted_iota(jnp.int32, (SGU_CHUNK, SGU_CHUNK), 0)
        j = lax.broadcasted_iota(jnp.int32, (SGU_CHUNK, SGU_CHUNK), 1)
        mask = i >= j
        if sample_kind:
            mask = mask & ((i // dec_seq) == (j // dec_seq))
        reps = SGU_CHUNK // dec_seq
        bcol = bcol_ref[int(sample_kind)]
        for g in range(n_groups):
            cols = slice(g * SGU_GROUP_DIM, (g + 1) * SGU_GROUP_DIM)
            if sample_kind:
                w = ws_ref[g, :dec_seq, :dec_seq]
                w = jnp.concatenate([w] * reps, axis=0)
                w = jnp.concatenate([w] * reps, axis=1)
            else:
                w = ws_ref[g]
            mixw_ref[g] = jnp.where(mask, w, 0.0).astype(BF16)
            mixb_ref[:, cols] = jnp.broadcast_to(bcol[:, g:g + 1],
                                                 (SGU_CHUNK, SGU_GROUP_DIM))

    pl.when(m == 0)(lambda: build_mixing(False))
    pl.when(m == n_prompt_blocks)(lambda: build_mixing(True))

    def gated_chunk(c):
        rows = pl.ds(c * SGU_CHUNK, SGU_CHUNK)
        v = v_ref[rows, :].astype(F32)
        mu = jnp.mean(v, axis=-1, keepdims=True)
        vc = v - mu
        vn = vc * lax.rsqrt(jnp.mean(vc * vc, axis=-1, keepdims=True) + EPS)
        vn = vn * lng_ref[...] + lnb_ref[...]
        vn_ref[rows, :] = vn
        vnb = vn.astype(BF16)

        gated = []
        for g in range(n_groups):
            cols = slice(g * SGU_GROUP_DIM, (g + 1) * SGU_GROUP_DIM)
            mixed = _dot(mixw_ref[g], vnb[:, cols]) + mixb_ref[:, cols]
            gated.append((u_ref[rows, cols].astype(F32) * mixed).astype(BF16))
        return jnp.concatenate(gated, axis=1)

    for s in range(tm // sub):
        chunks = range(s * sub // SGU_CHUNK, (s + 1) * sub // SGU_CHUNK)
        y = _dot(jnp.concatenate([gated_chunk(c) for c in chunks], axis=0), wbf_ref[...])
        _norm_tail(y, s * sub, x_ref, xo_ref, ho_ref, gpost_ref, gnext_ref)


def _sgu_out(zz, ln_g, ln_b, w_s, bcol, w_out, x, gpost, gnext,
             *, layer, mp, dec_seq, tm, sub):
    m_all, d = x.shape
    ms = m_all - mp
    npb, nsb = mp // tm, ms // tm
    n_groups = d // SGU_GROUP_DIM
    row_block = lambda m: (m, 0)
    const2 = lambda m: (0, 0)
    kernel = functools.partial(_sgu_out_kernel, n_prompt_blocks=npb, dec_seq=dec_seq,
                               sub=sub)
    return pl.pallas_call(
        kernel,
        grid=(npb + nsb,),
        in_specs=[
            pl.BlockSpec((tm, d), lambda m: (m, 0)),
            pl.BlockSpec((tm, d), lambda m: (m, 1)),
            _resident((1, d), const2),
            _resident((1, d), const2),
            _resident((None, n_groups, SGU_CHUNK, SGU_CHUNK),
                      lambda m: (layer, 0, 0, 0)),
            _resident((2, SGU_CHUNK, n_groups), lambda m: (0, 0, 0)),
            _resident((d, d), const2),
            pl.BlockSpec((tm, d), row_block),
            _resident((1, d), const2),
            _resident((1, d), const2),
        ],
        out_specs=[
            pl.BlockSpec((tm, d), row_block),
            pl.BlockSpec((tm, d), row_block),
            pl.BlockSpec((tm, d), lambda m: (jnp.maximum(m - npb, 0), 0)),
        ],
        out_shape=[
            jax.ShapeDtypeStruct((m_all, d), F32),
            jax.ShapeDtypeStruct((m_all, d), BF16),
            jax.ShapeDtypeStruct((ms, d), F32),
        ],
        scratch_shapes=[pltpu.VMEM((n_groups, SGU_CHUNK, SGU_CHUNK), BF16),
                        pltpu.VMEM((SGU_CHUNK, d), F32)],
        compiler_params=_params(1),
        name="sgu_out",
    )(zz, zz, ln_g, ln_b, w_s, bcol, w_out, x, gpost, gnext)


TM_CONV = 1024
TM_SAMPLE = 512
SUB_CONV = 512
TM_FFN = 2176
SUB_FFN = 272
SUB_SGU = 1088
TAIL_ROWS = 32
TM_TAIL = 512
SUB_TAIL = 256
TM_DOWN = 512
SUB_DOWN = 256
TN_CONV = 512
TN_FFN = 512
TN_SGU = 1024


def kernel(x_prompt, x_sample, cache_conv, norm_mix_pre, norm_mix_post, norm_ffn_pre, norm_ffn_post, a_w_in, a_conv_w, a_w_out, b_w_in, b_b_in, b_ln_g, b_ln_b, b_w_s, b_b_s, b_w_out, ffn_w_gate, ffn_w_up, ffn_w_down):
    batch, seq, d = x_prompt.shape
    dec_batch, dec_seq, _ = x_sample.shape
    depth = norm_mix_pre.shape[0]
    mp, ms = batch * seq, dec_batch * dec_seq
    assert depth == 2 and seq % TM_CONV == 0 and ms % TM_SAMPLE == 0
    assert SGU_CHUNK % dec_seq == 0 and SUB_CONV % dec_seq == 0 and dec_seq >= CONV_W - 1
    assert mp % TM_TAIL == 0 and ms % TM_TAIL == 0
    assert seq % SGU_CHUNK == 0 and SUB_DOWN % SGU_CHUNK == 0 and ms % TM_DOWN == 0
    assert (mp + ms) % TM_FFN == 0

    xp = x_prompt.reshape(mp, d)
    xs = x_sample.reshape(ms, d)
    row = lambda p, i: p[i].reshape(1, -1)

    conv_args = dict(layer=0, seq=seq, dec_seq=dec_seq, tn=TN_CONV, sub=SUB_CONV)
    p_p, tails, w_out_bf = _conv_in(xp, row(norm_mix_pre, 0), a_w_in, a_conv_w[0], None,
                                    a_w_out, tm=TM_CONV, **conv_args)
    hist = jnp.pad(cache_conv[0], ((0, 0), (0, dec_seq - (CONV_W - 1)), (0, 0)))
    p_s, cz_s = _conv_in(xs, row(norm_mix_pre, 0), a_w_in, a_conv_w[0],
                         hist.reshape(ms, d), None, tm=TM_SAMPLE, **conv_args)
    state_conv_prompt = tails[None, :, SUBLANES - (CONV_W - 1):, :]
    state_conv_sample = cz_s.reshape(dec_batch, dec_seq, d)[None, :, dec_seq - (CONV_W - 1):, :]
    x, h = _proj_out((p_p, p_s), w_out_bf, (xp, xs), row(norm_mix_post, 0),
                     row(norm_ffn_pre, 0), mp=mp, tm=TM_TAIL, sub=SUB_TAIL)

    a, w_down_bf = _ffn_up(h, ffn_w_gate, ffn_w_up, ffn_w_down, layer=0,
                           tm=TM_FFN, tn=TN_FFN, sub=SUB_FFN)
    x, h = _proj_out(a, w_down_bf, x, row(norm_ffn_post, 0),
                     row(norm_mix_pre, 1), mp=mp, tm=TM_DOWN, sub=SUB_DOWN)

    zz, w_out_bf = _sgu_in(h, b_w_in, b_b_in[:, None, :], b_w_out, layer=0,
                           tm=TM_FFN, tn=TN_SGU, sub=SUB_SGU)
    reps = SGU_CHUNK // dec_seq
    b_s = b_b_s[0]
    b_rows = jnp.stack([b_s, jnp.tile(b_s[:, :dec_seq], (1, reps))])
    bcol = jnp.swapaxes(b_rows, 1, 2)
    x, h, vn_s = _sgu_out(zz, b_ln_g[0].reshape(1, -1), b_ln_b[0].reshape(1, -1),
                          b_w_s, bcol, w_out_bf, x,
                          row(norm_mix_post, 1), row(norm_ffn_pre, 1),
                          layer=0, mp=mp, dec_seq=dec_seq, tm=TM_DOWN, sub=SUB_DOWN)
    state_sgu_v_sample = vn_s.reshape(1, dec_batch, dec_seq, d)

    a, w_down_bf = _ffn_up(h, ffn_w_gate, ffn_w_up, ffn_w_down, layer=1,
                           tm=TM_FFN, tn=TN_FFN, sub=SUB_FFN)
    yp, ys = _proj_out(a, w_down_bf, x, row(norm_ffn_post, 1),
                       None, mp=mp, tm=TM_DOWN, sub=SUB_DOWN)

    return (yp.reshape(batch, seq, d), ys.reshape(dec_batch, dec_seq, d),
            state_conv_prompt, state_conv_sample, state_sgu_v_sample)
```
